```python
import math
import jax, jax.numpy as jnp
from jax import lax
import numpy as np

D_MODEL = 4096
BATCH = 4
SEQ = 2048
DEPTH = 2
DEC_BATCH = 8
DEC_SEQ = 1
PAST_LEN = 16384
PAGE_SIZE = 128

N_HEADS = 8
HEAD_DIM = 128
ATTN_Q_WIDTH = 2 * N_HEADS * HEAD_DIM
ATTN_V_DIM = 2 * HEAD_DIM
ATTN_V_WIDTH = N_HEADS * ATTN_V_DIM
ROPE_THETA = 10000.0
SUBLN_EPS = 1e-5
POOL_WINDOWS = (2, 4, 8, 16)
POOL_GROUPS = len(POOL_WINDOWS)
POOL_WIDTH = D_MODEL // 4
POOL_GROUP_WIDTH = POOL_WIDTH // POOL_GROUPS
POOL_BUF = max(POOL_WINDOWS) - 1
SSM_WIDTH = D_MODEL // 4
SSM_GROUP_CH = 16
SSM_GROUPS = SSM_WIDTH // SSM_GROUP_CH
SSM_STATE = 64
N_BRANCHES = 3
MIX_WIDTH = POOL_WIDTH + ATTN_V_WIDTH + SSM_WIDTH
IN_WIDTH = POOL_WIDTH + 2 * ATTN_Q_WIDTH + ATTN_V_WIDTH + SSM_WIDTH + N_BRANCHES * D_MODEL
D_FF = ((8 * D_MODEL // 3 + 255) // 256) * 256
NORM_EPS = 1e-6

kernel_name = 'hybrid_pool_diffattn_s5_decode_step'


def rmsnorm(x, w, eps=NORM_EPS):
    xf = x.astype(jnp.float32)
    xf = xf * lax.rsqrt(jnp.mean(xf * xf, axis=-1, keepdims=True) + eps)
    return (xf * w.astype(jnp.float32)).astype(x.dtype)


def rope(x, pos):
    half = HEAD_DIM // 2
    inv = ROPE_THETA ** (-jnp.arange(half, dtype=jnp.float32) / half)
    ang = pos.astype(jnp.float32)[:, None] * inv[None, :]
    cos = jnp.cos(ang)[None, :, None, :]
    sin = jnp.sin(ang)[None, :, None, :]
    xf = x.astype(jnp.float32)
    x1, x2 = xf[..., :half], xf[..., half:]
    return jnp.concatenate([x1 * cos - x2 * sin, x2 * cos + x1 * sin], axis=-1).astype(x.dtype)


def pool_mixer(p, buf, start, w_pool, scale):
    B_, S_, _ = p.shape
    ext = jnp.concatenate([buf.astype(p.dtype), p], axis=1)
    cs = jnp.cumsum(ext.astype(jnp.float32), axis=1)
    cs = jnp.pad(cs, ((0, 0), (1, 0), (0, 0)))
    pos = start + jnp.arange(S_, dtype=jnp.int32)
    pf = p.astype(jnp.float32)
    end = cs[:, POOL_BUF + 1:]
    feats = []
    for g, w in enumerate(POOL_WINDOWS):
        cols = slice(g * POOL_GROUP_WIDTH, (g + 1) * POOL_GROUP_WIDTH)
        win_sum = end[..., cols] - cs[:, POOL_BUF + 1 - w:POOL_BUF + 1 - w + S_, cols]
        cnt = jnp.minimum(pos + 1, w).astype(jnp.float32)[None, :, None]
        feats.append(win_sum / cnt - pf[..., cols])
    f = jnp.stack(feats, axis=2)
    o = jnp.einsum('bsgc,gcd->bsgd', f, w_pool.astype(jnp.float32)).reshape(B_, S_, POOL_WIDTH)
    o = o * scale.astype(jnp.float32)
    return o.astype(p.dtype), ext[:, -POOL_BUF:]


def diff_attention(q, k, v, q_pos, k_pos, lam, lam_init, subln_w):
    B_, Sq = q.shape[:2]
    blk = 128 if Sq % 128 == 0 else Sq
    nb = Sq // blk
    scale = HEAD_DIM ** -0.5
    qb = jnp.moveaxis(q.reshape(B_, nb, blk, 2 * N_HEADS, HEAD_DIM), 1, 0)
    pb = q_pos.reshape(nb, blk)

    def one_block(args):
        qi, pi = args
        s = jnp.einsum('bqhd,bkhd->bhqk', qi, k, preferred_element_type=jnp.float32) * scale
        s = jnp.where((k_pos[None, :] <= pi[:, None])[None, None], s, -jnp.inf)
        a = jax.nn.softmax(s, axis=-1).reshape(B_, N_HEADS, 2, blk, -1)
        w = a[:, :, 0] - lam * a[:, :, 1]
        return jnp.einsum('bhqk,bkhe->bqhe', w.astype(v.dtype), v, preferred_element_type=jnp.float32)

    o = lax.map(one_block, (qb, pb))
    o = jnp.moveaxis(o, 0, 1).reshape(B_, Sq, N_HEADS, ATTN_V_DIM)
    o = rmsnorm(o, subln_w, SUBLN_EPS) * (1.0 - lam_init)
    return o.reshape(B_, Sq, ATTN_V_WIDTH).astype(q.dtype)


def _linear_combine(left, right):
    a_l, b_l = left
    a_r, b_r = right
    return a_l * a_r, a_r * b_l + b_r


def s5_mixer(u, h0_re, h0_im, prm):
    B_, S_, _ = u.shape
    f32 = jnp.float32
    uf = u.astype(f32)
    ug = uf.reshape(B_, S_, SSM_GROUPS, SSM_GROUP_CH).astype(jnp.complex64)
    a = lax.complex(prm['ssm_a_re'].astype(f32), prm['ssm_a_im'].astype(f32))
    dt = jnp.exp(prm['ssm_log_dt'].astype(f32))[:, None]
    a_bar = jnp.exp(a * dt)
    b = lax.complex(prm['ssm_b_re'].astype(f32), prm['ssm_b_im'].astype(f32))
    b_bar = ((a_bar - 1.0) / a)[..., None] * b
    bu = jnp.einsum('gpc,bsgc->bsgp', b_bar, ug)
    a_seq = jnp.broadcast_to(a_bar, bu.shape)
    a_cum, hs = lax.associative_scan(_linear_combine, (a_seq, bu), axis=1)
    h0 = lax.complex(h0_re.astype(f32), h0_im.astype(f32))
    hs = hs + a_cum * h0[:, None]
    c = lax.complex(prm['ssm_c_re'].astype(f32), prm['ssm_c_im'].astype(f32))
    y = jnp.real(jnp.einsum('gcp,bsgp->bsgc', c, hs)).reshape(B_, S_, SSM_WIDTH)
    y = y + prm['ssm_d'].astype(f32) * uf
    z = jax.nn.gelu(y)
    out = z * jax.nn.sigmoid(z @ prm['glu_w'].astype(f32) + prm['glu_b'].astype(f32))
    h_last = hs[:, -1]
    return out.astype(u.dtype), jnp.real(h_last), jnp.imag(h_last)


def decoder_layer(x, start, pool_buf, ssm_re, ssm_im, kv_past, lam_init, prm):
    B_, S_, _ = x.shape
    h = rmsnorm(x, prm['norm_mix_pre'])
    z = jnp.einsum('bsd,de->bse', h, prm['w_in'])
    offs = np.cumsum([0, POOL_WIDTH, ATTN_Q_WIDTH, ATTN_Q_WIDTH, ATTN_V_WIDTH, SSM_WIDTH, N_BRANCHES * D_MODEL])
    p_in, q, k, v, u, gates = [z[..., int(offs[i]):int(offs[i + 1])] for i in range(6)]
    pos = start + jnp.arange(S_, dtype=jnp.int32)
    q = rope(q.reshape(B_, S_, 2 * N_HEADS, HEAD_DIM), pos)
    k = rope(k.reshape(B_, S_, 2 * N_HEADS, HEAD_DIM), pos)
    v = v.reshape(B_, S_, N_HEADS, ATTN_V_DIM)
    o_pool, new_buf = pool_mixer(p_in, pool_buf, start, prm['pool_w'], prm['pool_scale'])
    if kv_past is None:
        k_all, v_all = k, v
    else:
        k_all = jnp.concatenate([kv_past[0].astype(k.dtype), k], axis=1)
        v_all = jnp.concatenate([kv_past[1].astype(v.dtype), v], axis=1)
    k_pos = jnp.arange(k_all.shape[1], dtype=jnp.int32)
    f32 = jnp.float32
    lam = (jnp.exp(jnp.sum(prm['lambda_q1'].astype(f32) * prm['lambda_k1'].astype(f32)))
           - jnp.exp(jnp.sum(prm['lambda_q2'].astype(f32) * prm['lambda_k2'].astype(f32))) + lam_init)
    o_attn = diff_attention(q, k_all, v_all, pos, k_pos, lam, lam_init, prm['subln_w'])
    o_ssm, new_re, new_im = s5_mixer(u, ssm_re, ssm_im, prm)
    wl = prm['w_lift']
    u_pool = o_pool @ wl[:POOL_WIDTH]
    u_attn = o_attn @ wl[POOL_WIDTH:POOL_WIDTH + ATTN_V_WIDTH]
    u_ssm = o_ssm @ wl[POOL_WIDTH + ATTN_V_WIDTH:]
    g = jax.nn.sigmoid(gates.reshape(B_, S_, N_BRANCHES, D_MODEL))
    merged = g[:, :, 0] * u_pool + g[:, :, 1] * u_attn + g[:, :, 2] * u_ssm
    x = x + rmsnorm(merged @ prm['w_out'], prm['norm_mix_post'])
    h2 = rmsnorm(x, prm['norm_ffn_pre'])
    f = (jax.nn.silu(h2 @ prm['ffn_w_gate']) * (h2 @ prm['ffn_w_up'])) @ prm['ffn_w_down']
    x = x + rmsnorm(f, prm['norm_ffn_post'])
    return x, k, v, new_buf, new_re, new_im


def setup_inputs(seed: int = 0) -> dict:
    key = jax.random.key(seed)
    ks = jax.random.split(key, 40)
    f32 = jnp.float32
    n_pages = PAST_LEN // PAGE_SIZE
    n_used = DEC_BATCH * n_pages
    n_pool = n_used + n_used // 4

    def nrm(k, shape, scale=1.0):
        return jax.random.normal(k, shape, f32) * scale

    def gain(k, shape):
        return 1.0 + 0.05 * jax.random.normal(k, shape, f32)

    page_table = jax.random.permutation(ks[7], n_pool)[:n_used].reshape(DEC_BATCH, n_pages).astype(jnp.int32)
    a_im = jnp.pi * jnp.arange(SSM_STATE, dtype=f32)[None, None, :] + 0.01 * nrm(ks[21], (DEPTH, SSM_GROUPS, SSM_STATE))
    return {
        'x_prompt': nrm(ks[0], (BATCH, SEQ, D_MODEL)),
        'x_sample': nrm(ks[1], (DEC_BATCH, DEC_SEQ, D_MODEL)),
        'cache_k': nrm(ks[2], (DEPTH, n_pool, PAGE_SIZE, 2 * N_HEADS, HEAD_DIM)),
        'cache_v': nrm(ks[3], (DEPTH, n_pool, PAGE_SIZE, N_HEADS, ATTN_V_DIM)),
        'state_pool': nrm(ks[4], (DEPTH, DEC_BATCH, POOL_BUF, POOL_WIDTH)),
        'state_ssm_re': nrm(ks[5], (DEPTH, DEC_BATCH, SSM_GROUPS, SSM_STATE), 0.5),
        'state_ssm_im': nrm(ks[6], (DEPTH, DEC_BATCH, SSM_GROUPS, SSM_STATE), 0.5),
        'page_table': page_table,
        'norm_mix_pre': gain(ks[8], (DEPTH, D_MODEL)),
        'norm_mix_post': gain(ks[9], (DEPTH, D_MODEL)),
        'norm_ffn_pre': gain(ks[10], (DEPTH, D_MODEL)),
        'norm_ffn_post': gain(ks[11], (DEPTH, D_MODEL)),
        'w_in': nrm(ks[12], (DEPTH, D_MODEL, IN_WIDTH), D_MODEL ** -0.5),
        'pool_w': nrm(ks[13], (DEPTH, POOL_GROUPS, POOL_GROUP_WIDTH, POOL_GROUP_WIDTH), POOL_GROUP_WIDTH ** -0.5),
        'pool_scale': gain(ks[14], (DEPTH, POOL_WIDTH)),
        'lambda_q1': nrm(ks[15], (DEPTH, HEAD_DIM), 0.1),
        'lambda_k1': nrm(ks[16], (DEPTH, HEAD_DIM), 0.1),
        'lambda_q2': nrm(ks[17], (DEPTH, HEAD_DIM), 0.1),
        'lambda_k2': nrm(ks[18], (DEPTH, HEAD_DIM), 0.1),
        'subln_w': gain(ks[19], (DEPTH, ATTN_V_DIM)),
        'ssm_a_re': -0.5 + 0.01 * nrm(ks[20], (DEPTH, SSM_GROUPS, SSM_STATE)),
        'ssm_a_im': a_im,
        'ssm_b_re': nrm(ks[22], (DEPTH, SSM_GROUPS, SSM_STATE, SSM_GROUP_CH), (2 * SSM_GROUP_CH) ** -0.5),
        'ssm_b_im': nrm(ks[23], (DEPTH, SSM_GROUPS, SSM_STATE, SSM_GROUP_CH), (2 * SSM_GROUP_CH) ** -0.5),
        'ssm_c_re': nrm(ks[24], (DEPTH, SSM_GROUPS, SSM_GROUP_CH, SSM_STATE), SSM_STATE ** -0.5),
        'ssm_c_im': nrm(ks[25], (DEPTH, SSM_GROUPS, SSM_GROUP_CH, SSM_STATE), SSM_STATE ** -0.5),
        'ssm_d': nrm(ks[26], (DEPTH, SSM_WIDTH)),
        'ssm_log_dt': jax.random.uniform(ks[27], (DEPTH, SSM_GROUPS), f32, math.log(0.001), math.log(0.1)),
        'glu_w': nrm(ks[28], (DEPTH, SSM_WIDTH, SSM_WIDTH), SSM_WIDTH ** -0.5),
        'glu_b': nrm(ks[29], (DEPTH, SSM_WIDTH), 0.01),
        'w_lift': nrm(ks[30], (DEPTH, MIX_WIDTH, D_MODEL), POOL_WIDTH ** -0.5),
        'w_out': nrm(ks[31], (DEPTH, D_MODEL, D_MODEL), D_MODEL ** -0.5),
        'ffn_w_gate': nrm(ks[32], (DEPTH, D_MODEL, D_FF), D_MODEL ** -0.5),
        'ffn_w_up': nrm(ks[33], (DEPTH, D_MODEL, D_FF), D_MODEL ** -0.5),
        'ffn_w_down': nrm(ks[34], (DEPTH, D_FF, D_MODEL), D_FF ** -0.5),
    }


def reference(x_prompt, x_sample, cache_k, cache_v, state_pool, state_ssm_re, state_ssm_im, page_table,
              norm_mix_pre, norm_mix_post, norm_ffn_pre, norm_ffn_post, w_in, pool_w, pool_scale,
              lambda_q1, lambda_k1, lambda_q2, lambda_k2, subln_w, ssm_a_re, ssm_a_im, ssm_b_re, ssm_b_im,
              ssm_c_re, ssm_c_im, ssm_d, ssm_log_dt, glu_w, glu_b, w_lift, w_out,
              ffn_w_gate, ffn_w_up, ffn_w_down):
    n_past = page_table.shape[1] * PAGE_SIZE
    dec_b = x_sample.shape[0]
    bp = x_prompt.shape[0]
    xp, xs = x_prompt, x_sample
    kp_l, vp_l, pp_l, rp_l, ip_l = [], [], [], [], []
    ks_l, vs_l, ps_l, rs_l, is_l = [], [], [], [], []
    for l in range(DEPTH):
        prm = {
            'norm_mix_pre': norm_mix_pre[l], 'norm_mix_post': norm_mix_post[l],
            'norm_ffn_pre': norm_ffn_pre[l], 'norm_ffn_post': norm_ffn_post[l],
            'w_in': w_in[l], 'pool_w': pool_w[l], 'pool_scale': pool_scale[l],
            'lambda_q1': lambda_q1[l], 'lambda_k1': lambda_k1[l],
            'lambda_q2': lambda_q2[l], 'lambda_k2': lambda_k2[l], 'subln_w': subln_w[l],
            'ssm_a_re': ssm_a_re[l], 'ssm_a_im': ssm_a_im[l], 'ssm_b_re': ssm_b_re[l], 'ssm_b_im': ssm_b_im[l],
            'ssm_c_re': ssm_c_re[l], 'ssm_c_im': ssm_c_im[l], 'ssm_d': ssm_d[l], 'ssm_log_dt': ssm_log_dt[l],
            'glu_w': glu_w[l], 'glu_b': glu_b[l], 'w_lift': w_lift[l], 'w_out': w_out[l],
            'ffn_w_gate': ffn_w_gate[l], 'ffn_w_up': ffn_w_up[l], 'ffn_w_down': ffn_w_down[l],
        }
        lam_init = 0.8 - 0.6 * math.exp(-0.3 * l)
        zero_buf = jnp.zeros((bp, POOL_BUF, POOL_WIDTH), xp.dtype)
        zero_h = jnp.zeros((bp, SSM_GROUPS, SSM_STATE), jnp.float32)
        xp, kp, vp, pbuf, hre, him = decoder_layer(xp, 0, zero_buf, zero_h, zero_h, None, lam_init, prm)
        kp_l.append(kp); vp_l.append(vp); pp_l.append(pbuf); rp_l.append(hre); ip_l.append(him)
        k_past = cache_k[l, page_table].reshape(dec_b, n_past, 2 * N_HEADS, HEAD_DIM)
        v_past = cache_v[l, page_table].reshape(dec_b, n_past, N_HEADS, ATTN_V_DIM)
        xs, ksn, vsn, sbuf, sre, sim = decoder_layer(xs, n_past, state_pool[l], state_ssm_re[l], state_ssm_im[l],
                                                     (k_past, v_past), lam_init, prm)
        ks_l.append(ksn); vs_l.append(vsn); ps_l.append(sbuf); rs_l.append(sre); is_l.append(sim)
    return (xp, xs,
            jnp.stack(kp_l), jnp.stack(vp_l), jnp.stack(pp_l), jnp.stack(rp_l), jnp.stack(ip_l),
            jnp.stack(ks_l), jnp.stack(vs_l), jnp.stack(ps_l), jnp.stack(rs_l), jnp.stack(is_l))
```

```python
import functools
import math

import jax
import jax.numpy as jnp
from jax import lax
from jax.experimental import pallas as pl
from jax.experimental.pallas import tpu as pltpu

F32 = jnp.float32
BF16 = jnp.bfloat16

V7X_VMEM_BYTES = 64 * 1024 * 1024
VMEM_LIMIT_BYTES = V7X_VMEM_BYTES - 8 * 1024 * 1024
SUBLANES = 8
LANES = 128
BF16_ROWS = 16

NORM_EPS = 1e-6
SUBLN_EPS = 1e-5
ROPE_THETA = 10000.0
POOL_WINDOWS = (2, 4, 8, 16)
POOL_BUF = max(POOL_WINDOWS) - 1
SSM_BLOCKS = 4
ATTN_TILE = 512
S5_TIME_CHUNK = 64


def _params(*sem):
    return pltpu.CompilerParams(dimension_semantics=sem, vmem_limit_bytes=VMEM_LIMIT_BYTES)


def _tile(n, pref):
    t = min(n, pref)
    while n % t:
        t //= 2
    return t


def _rms(x, w, eps):
    return x * lax.rsqrt(jnp.mean(x * x, axis=-1, keepdims=True) + eps) * w


def _rmsnorm_kernel(x_ref, w_ref, o_ref):
    o_ref[...] = _rms(x_ref[...], w_ref[...], NORM_EPS).astype(o_ref.dtype)


def rmsnorm_bf16(x, w):
    m, d = x.shape
    tm = _tile(m, 256)
    return pl.pallas_call(
        _rmsnorm_kernel,
        grid=(m // tm,),
        in_specs=[pl.BlockSpec((tm, d), lambda i: (i, 0)), pl.BlockSpec((1, d), lambda i: (0, 0))],
        out_specs=pl.BlockSpec((tm, d), lambda i: (i, 0)),
        out_shape=jax.ShapeDtypeStruct((m, d), BF16),
        compiler_params=_params("parallel"),
        name="rmsnorm",
    )(x, w.reshape(1, d))


def _resnorm_kernel(x_ref, y_ref, wpost_ref, wnext_ref, xo_ref, ho_ref):
    x = x_ref[...] + _rms(y_ref[...], wpost_ref[...], NORM_EPS)
    xo_ref[...] = x
    ho_ref[...] = _rms(x, wnext_ref[...], NORM_EPS).astype(ho_ref.dtype)


def _resnorm_last_kernel(x_ref, y_ref, wpost_ref, xo_ref):
    xo_ref[...] = x_ref[...] + _rms(y_ref[...], wpost_ref[...], NORM_EPS)


def residual_norm(x, y, w_post, w_next):
    m, d = x.shape
    tm = _tile(m, 256)
    row = pl.BlockSpec((tm, d), lambda i: (i, 0))
    vec = pl.BlockSpec((1, d), lambda i: (0, 0))
    if w_next is None:
        return pl.pallas_call(
            _resnorm_last_kernel, grid=(m // tm,), in_specs=[row, row, vec], out_specs=row,
            out_shape=jax.ShapeDtypeStruct((m, d), F32), compiler_params=_params("parallel"),
            name="resnorm_last",
        )(x, y, w_post.reshape(1, d)), None
    return pl.pallas_call(
        _resnorm_kernel, grid=(m // tm,), in_specs=[row, row, vec, vec], out_specs=[row, row],
        out_shape=[jax.ShapeDtypeStruct((m, d), F32), jax.ShapeDtypeStruct((m, d), BF16)],
        compiler_params=_params("parallel"), name="resnorm",
    )(x, y, w_post.reshape(1, d), w_next.reshape(1, d))


def _mm_kernel(x_ref, w_ref, o_ref):
    o_ref[...] = jnp.dot(x_ref[...], w_ref[...], preferred_element_type=F32).astype(o_ref.dtype)


def matmul(x, w, *, tm, tn, out_dtype=F32, n_blocks=None, w_col=None, out_shape=None, out_map=None):
    m, k = x.shape
    n = w.shape[1]
    tm = _tile(m, tm)
    n_blocks = n // tn if n_blocks is None else n_blocks
    w_col = (lambda j: j) if w_col is None else w_col
    out_map = (lambda i, j: (i, w_col(j))) if out_map is None else out_map
    out_shape = (m, n) if out_shape is None else out_shape
    return pl.pallas_call(
        _mm_kernel,
        grid=(m // tm, n_blocks),
        in_specs=[pl.BlockSpec((tm, k), lambda i, j: (i, 0)),
                  pl.BlockSpec((k, tn), lambda i, j: (0, w_col(j)))],
        out_specs=pl.BlockSpec((tm, tn), out_map),
        out_shape=jax.ShapeDtypeStruct(out_shape, out_dtype),
        compiler_params=_params("parallel", "arbitrary"),
        name="matmul",
    )(x, w)


def _mm_acc_kernel(x_ref, w_ref, o_ref, acc_ref):
    kk = pl.program_id(2)

    @pl.when(kk == 0)
    def _():
        acc_ref[...] = jnp.zeros_like(acc_ref)

    acc_ref[...] += jnp.dot(x_ref[...], w_ref[...], preferred_element_type=F32)

    @pl.when(kk == pl.num_programs(2) - 1)
    def _():
        o_ref[...] = acc_ref[...]


def matmul_ksplit(x, w, *, tm, tn, tk):
    m, k = x.shape
    n = w.shape[1]
    tm = _tile(m, tm)
    return pl.pallas_call(
        _mm_acc_kernel,
        grid=(m // tm, n // tn, k // tk),
        in_specs=[pl.BlockSpec((tm, tk), lambda i, j, kk: (i, kk)),
                  pl.BlockSpec((tk, tn), lambda i, j, kk: (kk, j))],
        out_specs=pl.BlockSpec((tm, tn), lambda i, j, kk: (i, j)),
        out_shape=jax.ShapeDtypeStruct((m, n), F32),
        scratch_shapes=[pltpu.VMEM((tm, tn), F32)],
        compiler_params=_params("parallel", "parallel", "arbitrary"),
        name="matmul_ksplit",
    )(x, w)


def _swiglu_kernel(x_ref, wg_ref, wu_ref, o_ref):
    x = x_ref[...]
    g = jnp.dot(x, wg_ref[...], preferred_element_type=F32)
    u = jnp.dot(x, wu_ref[...], preferred_element_type=F32)
    o_ref[...] = (g * jax.nn.sigmoid(g) * u).astype(o_ref.dtype)


def swiglu(x, wg, wu, *, tm, tn):
    m, k = x.shape
    n = wg.shape[1]
    tm = _tile(m, tm)
    return pl.pallas_call(
        _swiglu_kernel,
        grid=(m // tm, n // tn),
        in_specs=[pl.BlockSpec((tm, k), lambda i, j: (i, 0)),
                  pl.BlockSpec((k, tn), lambda i, j: (0, j)),
                  pl.BlockSpec((k, tn), lambda i, j: (0, j))],
        out_specs=pl.BlockSpec((tm, tn), lambda i, j: (i, j)),
        out_shape=jax.ShapeDtypeStruct((m, n), BF16),
        compiler_params=_params("parallel", "arbitrary"),
        name="swiglu",
    )(x, wg, wu)


def _lift_kernel(op_ref, oa_ref, os_ref, wp_ref, wa_ref, ws_ref, g0_ref, g1_ref, g2_ref, o_ref):
    up = jnp.dot(op_ref[...], wp_ref[...], preferred_element_type=F32)
    ua = jnp.dot(oa_ref[...], wa_ref[...], preferred_element_type=F32)
    us = jnp.dot(os_ref[...], ws_ref[...], preferred_element_type=F32)
    o = (jax.nn.sigmoid(g0_ref[...]) * up + jax.nn.sigmoid(g1_ref[...]) * ua
         + jax.nn.sigmoid(g2_ref[...]) * us)
    o_ref[...] = o.astype(o_ref.dtype)


def lift_merge(o_pool, o_attn, o_ssm, ssm_map, wl_pool, wl_attn, wl_ssm, z, gate_col0, *, tm, tn):
    m = o_pool.shape[0]
    d = wl_pool.shape[1]
    tm = _tile(m, tm)
    gb, nb = gate_col0 // tn, d // tn

    def gate_spec(i_branch):
        return pl.BlockSpec((tm, tn), lambda i, j: (i, gb + i_branch * nb + j))

    return pl.pallas_call(
        _lift_kernel,
        grid=(m // tm, nb),
        in_specs=[pl.BlockSpec((tm, o_pool.shape[1]), lambda i, j: (i, 0)),
                  pl.BlockSpec((tm, o_attn.shape[1]), lambda i, j: (i, 0)),
                  pl.BlockSpec((tm, wl_ssm.shape[0]), lambda i, j: ssm_map(i)),
                  pl.BlockSpec((wl_pool.shape[0], tn), lambda i, j: (0, j)),
                  pl.BlockSpec((wl_attn.shape[0], tn), lambda i, j: (0, j)),
                  pl.BlockSpec((wl_ssm.shape[0], tn), lambda i, j: (0, j)),
                  gate_spec(0), gate_spec(1), gate_spec(2)],
        out_specs=pl.BlockSpec((tm, tn), lambda i, j: (i, j)),
        out_shape=jax.ShapeDtypeStruct((m, d), BF16),
        compiler_params=_params("parallel", "arbitrary"),
        name="lift_merge",
    )(o_pool, o_attn, o_ssm, wl_pool, wl_attn, wl_ssm, z, z, z)


def _rope_tables(pos, head_dim):
    half = head_dim // 2
    inv = ROPE_THETA ** (-jnp.arange(half, dtype=F32) / half)
    ang = pos.astype(F32)[:, None] * inv[None, :]
    cos, sin = jnp.cos(ang), jnp.sin(ang)
    return jnp.concatenate([cos, cos], axis=-1), jnp.concatenate([-sin, sin], axis=-1)


def _rope_kernel(q_ref, k_ref, v_ref, cos_ref, sin_ref, qo_ref, ko_ref, kf_ref, vo_ref, vf_ref, *, head_dim):
    cos, sin = cos_ref[...], sin_ref[...]
    for h in range(q_ref.shape[1] // head_dim):
        cols = slice(h * head_dim, (h + 1) * head_dim)
        for src, outs in ((q_ref, (qo_ref,)), (k_ref, (ko_ref, kf_ref))):
            x = src[:, cols]
            r = x * cos + pltpu.roll(x, head_dim // 2, 1) * sin
            for o in outs:
                o[:, cols] = r.astype(o.dtype)
    v = v_ref[...]
    vo_ref[...] = v.astype(vo_ref.dtype)
    vf_ref[...] = v


def rope_split(z, cos, sin, *, q_col0, width, head_dim, tm):
    m = z.shape[0]
    s_tab = cos.shape[0]
    tm = _tile(min(m, s_tab), tm)
    tc = 1024
    nc = width // tc
    qb = q_col0 // tc
    n_tab = s_tab // tm

    def zspec(off):
        return pl.BlockSpec((tm, tc), lambda i, j: (i, off + j))

    tab = pl.BlockSpec((tm, head_dim), lambda i, j: (i % n_tab, 0))
    out = pl.BlockSpec((tm, tc), lambda i, j: (i, j))
    return pl.pallas_call(
        functools.partial(_rope_kernel, head_dim=head_dim),
        grid=(m // tm, nc),
        in_specs=[zspec(qb), zspec(qb + nc), zspec(qb + 2 * nc), tab, tab],
        out_specs=[out] * 5,
        out_shape=[jax.ShapeDtypeStruct((m, width), dt) for dt in (BF16, BF16, F32, BF16, F32)],
        compiler_params=_params("parallel", "arbitrary"),
        name="rope_split",
    )(z, z, z, cos, sin)


def _pool_features(ext_load, p, pos0, gw):
    s = p.shape[0]
    pos = pos0 + lax.broadcasted_iota(jnp.int32, (s, 1), 0)
    feats = []
    for g, w in enumerate(POOL_WINDOWS):
        cols = slice(g * gw, (g + 1) * gw)
        win = p[:, cols]
        for j in range(1, w):
            win = win + ext_load(j, cols)
        cnt = jnp.minimum(pos + 1, w).astype(F32)
        feats.append(win / cnt - p[:, cols])
    return feats


def _pool_kernel(p_ref, w_ref, scale_ref, o_ref, buf_ref, ext_ref, *, s, gw):
    pad = BF16_ROWS
    ext_ref[0:pad, :] = jnp.zeros((pad, ext_ref.shape[1]), F32)
    ext_ref[pad:pad + s, :] = p_ref[...]
    p = p_ref[...]
    feats = _pool_features(lambda j, cols: ext_ref[pad - j:pad - j + s, cols], p, 0, gw)
    for g, f in enumerate(feats):
        cols = slice(g * gw, (g + 1) * gw)
        o = jnp.dot(f.astype(BF16), w_ref[g], preferred_element_type=F32) * scale_ref[:, cols]
        o_ref[:, cols] = o.astype(o_ref.dtype)
    buf_ref[...] = ext_ref[pad + s - POOL_BUF:pad + s, :]


def pool_mixer_prompt(z3, pool_w, pool_scale):
    b, s, _ = z3.shape
    ng, gw, _ = pool_w.shape
    width = ng * gw
    return pl.pallas_call(
        functools.partial(_pool_kernel, s=s, gw=gw),
        grid=(b,),
        in_specs=[pl.BlockSpec((None, s, width), lambda i: (i, 0, 0)),
                  pl.BlockSpec((ng, gw, gw), lambda i: (0, 0, 0)),
                  pl.BlockSpec((1, width), lambda i: (0, 0))],
        out_specs=[pl.BlockSpec((None, s, width), lambda i: (i, 0, 0)),
                   pl.BlockSpec((None, POOL_BUF, width), lambda i: (i, 0, 0))],
        out_shape=[jax.ShapeDtypeStruct((b, s, width), BF16),
                   jax.ShapeDtypeStruct((b, POOL_BUF, width), F32)],
        scratch_shapes=[pltpu.VMEM((BF16_ROWS + s, width), F32)],
        compiler_params=_params("parallel"),
        name="pool_mixer",
    )(z3, pool_w, pool_scale.reshape(1, width))


def _s5_discretize(a_re, a_im, log_dt, b_re, b_im, c_re, c_im):
    g, p = a_re.shape
    c = b_re.shape[-1]
    a = lax.complex(a_re.astype(F32), a_im.astype(F32))
    dt = jnp.exp(log_dt.astype(F32))[:, None]
    a_bar = jnp.exp(a * dt)
    b_bar = ((a_bar - 1.0) / a)[..., None] * lax.complex(b_re.astype(F32), b_im.astype(F32))
    gb = g // SSM_BLOCKS
    eye = jnp.eye(gb, dtype=F32)

    def b_blocks(x):
        x = x.reshape(SSM_BLOCKS, gb, p, c)
        return jnp.einsum('kgpc,gh->kgchp', x, eye).reshape(SSM_BLOCKS, gb * c, gb * p).astype(BF16)

    def c_blocks(x):
        x = x.reshape(SSM_BLOCKS, gb, c, p)
        return jnp.einsum('kgcp,gh->kgphc', x, eye).reshape(SSM_BLOCKS, gb * p, gb * c).astype(BF16)

    return (jnp.real(a_bar).reshape(1, g * p), jnp.imag(a_bar).reshape(1, g * p),
            b_blocks(jnp.real(b_bar)), b_blocks(jnp.imag(b_bar)),
            c_blocks(c_re.astype(F32)), c_blocks(-c_im.astype(F32)))


def _s5_input_proj(u_bf16, br_ref, bi_ref, store_re, store_im):
    cw, sw = br_ref.shape[1], br_ref.shape[2]
    for k in range(SSM_BLOCKS):
        uk = u_bf16[:, k * cw:(k + 1) * cw]
        store_re(k * sw, sw, jnp.dot(uk, br_ref[k], preferred_element_type=F32))
        store_im(k * sw, sw, jnp.dot(uk, bi_ref[k], preferred_element_type=F32))


def _s5_output(hs_re, hs_im, u, cr_ref, ci_ref, d_ref, gw_ref, gb_ref):
    sw, cw = cr_ref.shape[1], cr_ref.shape[2]
    ys = []
    for k in range(SSM_BLOCKS):
        ys.append(jnp.dot(hs_re(k * sw, sw).astype(BF16), cr_ref[k], preferred_element_type=F32)
                  + jnp.dot(hs_im(k * sw, sw).astype(BF16), ci_ref[k], preferred_element_type=F32))
    y = jnp.concatenate(ys, axis=-1) + d_ref[...] * u
    zz = jax.nn.gelu(y, approximate=True)
    gate = jnp.dot(zz.astype(BF16), gw_ref[...], preferred_element_type=F32) + gb_ref[...]
    return zz * jax.nn.sigmoid(gate)


def _s5_kernel(u_ref, h0r_ref, h0i_ref, ar_ref, ai_ref, br_ref, bi_ref, cr_ref, ci_ref, d_ref, gw_ref, gb_ref,
               o_ref, hr_ref, hi_ref, xr_sc, xi_sc, *, nb, lane_chunk):
    rows, n_state = xr_sc.shape
    steps = SUBLANES // nb

    @pl.when(pl.program_id(0) == 0)
    def _():
        hr_ref[...] = h0r_ref[...]
        hi_ref[...] = h0i_ref[...]

    u = u_ref[...]

    def store(sc):
        def f(c0, w, val):
            sc[:, c0:c0 + w] = val
        return f

    _s5_input_proj(u.astype(BF16), br_ref, bi_ref, store(xr_sc), store(xi_sc))

    row = lax.broadcasted_iota(jnp.int32, (SUBLANES, lane_chunk), 0)
    for c0 in range(0, n_state, lane_chunk):
        cols = slice(c0, c0 + lane_chunk)
        ar = jnp.broadcast_to(ar_ref[:, cols], (SUBLANES, lane_chunk))
        ai = jnp.broadcast_to(ai_ref[:, cols], (SUBLANES, lane_chunk))

        def tile(i, carry, cols=cols, ar=ar, ai=ai):
            hr, hi = carry
            r0 = pl.multiple_of(i * SUBLANES, SUBLANES)
            xr = xr_sc[pl.ds(r0, SUBLANES), cols]
            xi = xi_sc[pl.ds(r0, SUBLANES), cols]
            out_r, out_i = xr, xi
            for k in range(steps):
                pr = pltpu.roll(hr, nb, 0) if nb < SUBLANES else hr
                pi = pltpu.roll(hi, nb, 0) if nb < SUBLANES else hi
                hr = ar * pr - ai * pi + xr
                hi = ar * pi + ai * pr + xi
                sel = (row >= k * nb) & (row < (k + 1) * nb)
                out_r = jnp.where(sel, hr, out_r)
                out_i = jnp.where(sel, hi, out_i)
            xr_sc[pl.ds(r0, SUBLANES), cols] = out_r
            xi_sc[pl.ds(r0, SUBLANES), cols] = out_i
            return hr, hi

        hr, hi = lax.fori_loop(0, rows // SUBLANES, tile, (hr_ref[:, cols], hi_ref[:, cols]))
        hr_ref[:, cols] = hr
        hi_ref[:, cols] = hi

    out = _s5_output(lambda c0, w: xr_sc[:, c0:c0 + w], lambda c0, w: xi_sc[:, c0:c0 + w],
                     u, cr_ref, ci_ref, d_ref, gw_ref, gb_ref)
    o_ref[...] = out.astype(o_ref.dtype)


def s5_mixer_prompt(u_tb, h0_re, h0_im, prm, nb, *, t_chunk):
    n_rows, cw = u_tb.shape
    a_re, a_im, b_re, b_im, c_re, c_im, d, glu_w, glu_b = prm
    n_state = a_re.shape[1]
    rows = t_chunk * nb
    const2 = lambda i: (0, 0)
    const3 = lambda i: (0, 0, 0)
    return pl.pallas_call(
        functools.partial(_s5_kernel, nb=nb, lane_chunk=512),
        grid=(n_rows // rows,),
        in_specs=[pl.BlockSpec((rows, cw), lambda i: (i, 0)),
                  pl.BlockSpec((SUBLANES, n_state), const2), pl.BlockSpec((SUBLANES, n_state), const2),
                  pl.BlockSpec((1, n_state), const2), pl.BlockSpec((1, n_state), const2),
                  pl.BlockSpec(b_re.shape, const3), pl.BlockSpec(b_im.shape, const3),
                  pl.BlockSpec(c_re.shape, const3), pl.BlockSpec(c_im.shape, const3),
                  pl.BlockSpec((1, cw), const2), pl.BlockSpec((cw, cw), const2), pl.BlockSpec((1, cw), const2)],
        out_specs=[pl.BlockSpec((rows, cw), lambda i: (i, 0)),
                   pl.BlockSpec((SUBLANES, n_state), const2), pl.BlockSpec((SUBLANES, n_state), const2)],
        out_shape=[jax.ShapeDtypeStruct((n_rows, cw), BF16),
                   jax.ShapeDtypeStruct((SUBLANES, n_state), F32),
                   jax.ShapeDtypeStruct((SUBLANES, n_state), F32)],
        scratch_shapes=[pltpu.VMEM((rows, n_state), F32), pltpu.VMEM((rows, n_state), F32)],
        compiler_params=_params("arbitrary"),
        name="s5_mixer",
    )(u_tb, h0_re, h0_im, a_re, a_im, b_re, b_im, c_re, c_im, d, glu_w, glu_b)


def _decode_mix_kernel(p_ref, u_ref, buf_ref, pw_ref, ps_ref, h0r_ref, h0i_ref, ar_ref, ai_ref,
                       br_ref, bi_ref, cr_ref, ci_ref, d_ref, gw_ref, gb_ref,
                       op_ref, nbuf_ref, os_ref, hr_ref, hi_ref, *, pos0, gw):
    p = p_ref[...]
    feats = []
    for g, w in enumerate(POOL_WINDOWS):
        cols = slice(g * gw, (g + 1) * gw)
        win = p[:, cols]
        for j in range(1, w):
            win = win + buf_ref[POOL_BUF - j, :, cols]
        feats.append(win / float(min(pos0 + 1, w)) - p[:, cols])
    for g, f in enumerate(feats):
        cols = slice(g * gw, (g + 1) * gw)
        o = jnp.dot(f.astype(BF16), pw_ref[g], preferred_element_type=F32) * ps_ref[:, cols]
        op_ref[:, cols] = o
    for r in range(POOL_BUF - 1):
        nbuf_ref[r] = buf_ref[r + 1]
    nbuf_ref[POOL_BUF - 1] = p

    u = u_ref[...]

    def store(ref):
        def f(c0, w, val):
            ref[:, c0:c0 + w] = val
        return f

    _s5_input_proj(u.astype(BF16), br_ref, bi_ref, store(hr_ref), store(hi_ref))
    h0r, h0i = h0r_ref[...], h0i_ref[...]
    ar, ai = ar_ref[...], ai_ref[...]
    hr = ar * h0r - ai * h0i + hr_ref[...]
    hi = ar * h0i + ai * h0r + hi_ref[...]
    hr_ref[...] = hr
    hi_ref[...] = hi
    os_ref[...] = _s5_output(lambda c0, w: hr_ref[:, c0:c0 + w], lambda c0, w: hi_ref[:, c0:c0 + w],
                             u, cr_ref, ci_ref, d_ref, gw_ref, gb_ref)


def decode_mixers(z, u_col0, buf_t, pool_w, pool_scale, h0_re, h0_im, s5prm, pos0):
    r = z.shape[0]
    ng, gw, _ = pool_w.shape
    pw = ng * gw
    a_re, a_im, b_re, b_im, c_re, c_im, d, glu_w, glu_b = s5prm
    cw = d.shape[1]
    n_state = a_re.shape[1]
    full = lambda a: pl.BlockSpec(a.shape, lambda i: (0,) * a.ndim)
    args = (buf_t, pool_w, pool_scale.reshape(1, pw), h0_re, h0_im, a_re, a_im, b_re, b_im, c_re, c_im,
            d, glu_w, glu_b)
    return pl.pallas_call(
        functools.partial(_decode_mix_kernel, pos0=pos0, gw=gw),
        grid=(1,),
        in_specs=[pl.BlockSpec((r, pw), lambda i: (0, 0)),
                  pl.BlockSpec((r, cw), lambda i: (0, u_col0 // cw))] + [full(a) for a in args],
        out_specs=[pl.BlockSpec((r, pw), lambda i: (0, 0)),
                   pl.BlockSpec((POOL_BUF, r, pw), lambda i: (0, 0, 0)),
                   pl.BlockSpec((r, cw), lambda i: (0, 0)),
                   pl.BlockSpec((r, n_state), lambda i: (0, 0)),
                   pl.BlockSpec((r, n_state), lambda i: (0, 0))],
        out_shape=[jax.ShapeDtypeStruct((r, pw), F32),
                   jax.ShapeDtypeStruct((POOL_BUF, r, pw), F32),
                   jax.ShapeDtypeStruct((r, cw), F32),
                   jax.ShapeDtypeStruct((r, n_state), F32),
                   jax.ShapeDtypeStruct((r, n_state), F32)],
        compiler_params=_params("arbitrary"),
        name="decode_mixers",
    )(z, z, *args)


def _lambda(lq1_ref, lk1_ref, lq2_ref, lk2_ref, lam_init):
    return (jnp.exp(jnp.sum(lq1_ref[...] * lk1_ref[...], axis=-1, keepdims=True))
            - jnp.exp(jnp.sum(lq2_ref[...] * lk2_ref[...], axis=-1, keepdims=True)) + lam_init)


def _diff_finish(o1, o2, lam, sw, lam_init):
    o = o1 - lam * o2
    return _rms(o, sw, SUBLN_EPS) * (1.0 - lam_init)


def _attn_kernel(q_ref, k_ref, v_ref, lq1_ref, lk1_ref, lq2_ref, lk2_ref, sw_ref, o_ref,
                 m_sc, l_sc, acc_sc, *, t, hd, scale, lam_init):
    i = pl.program_id(2)
    q = q_ref[...]
    m_sc[...] = jnp.full(m_sc.shape, -jnp.inf, F32)
    l_sc[...] = jnp.zeros(l_sc.shape, F32)
    acc_sc[...] = jnp.zeros(acc_sc.shape, F32)

    def kv_tile(j, diagonal):
        r0 = pl.multiple_of(j * t, t)
        kt = k_ref[pl.ds(r0, t), :]
        vt = v_ref[pl.ds(r0, t), :]
        for mp in range(2):
            s = lax.dot_general(q[:, mp * hd:(mp + 1) * hd], kt[:, mp * hd:(mp + 1) * hd],
                                (((1,), (1,)), ((), ())), preferred_element_type=F32) * scale
            if diagonal:
                rr = lax.broadcasted_iota(jnp.int32, (t, t), 0)
                cc = lax.broadcasted_iota(jnp.int32, (t, t), 1)
                s = jnp.where(cc <= rr, s, -jnp.inf)
            m_old = m_sc[mp]
            m_new = jnp.maximum(m_old, jnp.max(s, axis=-1, keepdims=True))
            alpha = jnp.exp(m_old - m_new)
            p = jnp.exp(s - m_new)
            l_sc[mp] = alpha * l_sc[mp] + jnp.sum(p, axis=-1, keepdims=True)
            acc_sc[mp] = alpha * acc_sc[mp] + jnp.dot(p.astype(BF16), vt, preferred_element_type=F32)
            m_sc[mp] = m_new

    def body(j, c):
        kv_tile(j, False)
        return c

    lax.fori_loop(0, i, body, 0)
    kv_tile(i, True)
    lam = _lambda(lq1_ref, lk1_ref, lq2_ref, lk2_ref, lam_init)
    o = _diff_finish(acc_sc[0] / l_sc[0], acc_sc[1] / l_sc[1], lam, sw_ref[...], lam_init)
    o_ref[...] = o.astype(o_ref.dtype)


def diff_attention_prompt(q, k, v, lam_prm, subln_w, *, batch, lam_init, t):
    m, width = q.shape
    s = m // batch
    hd = lam_prm[0].shape[-1]
    vd = 2 * hd
    nh = width // vd
    t = _tile(s, t)
    nq = s // t
    vec = lambda n: pl.BlockSpec((1, n), lambda b, h, i: (0, 0))
    return pl.pallas_call(
        functools.partial(_attn_kernel, t=t, hd=hd, scale=hd ** -0.5, lam_init=lam_init),
        grid=(batch, nh, nq),
        in_specs=[pl.BlockSpec((t, vd), lambda b, h, i: (b * nq + i, h)),
                  pl.BlockSpec((s, vd), lambda b, h, i: (b, h)),
                  pl.BlockSpec((s, vd), lambda b, h, i: (b, h)),
                  vec(hd), vec(hd), vec(hd), vec(hd), vec(vd)],
        out_specs=pl.BlockSpec((t, vd), lambda b, h, i: (b * nq + i, h)),
        out_shape=jax.ShapeDtypeStruct((m, width), BF16),
        scratch_shapes=[pltpu.VMEM((2, t, 1), F32), pltpu.VMEM((2, t, 1), F32), pltpu.VMEM((2, t, vd), F32)],
        compiler_params=_params("parallel", "parallel", "arbitrary"),
        name="diff_attention",
    )(q, k, v, *[p.reshape(1, hd) for p in lam_prm], subln_w.reshape(1, vd))


def _dec_attn_kernel(pt_ref, qbd_ref, kc_ref, vc_ref, kn_ref, vn_ref, lq1_ref, lk1_ref, lq2_ref, lk2_ref,
                     sw_ref, o_ref, m_sc, l_sc, acc_sc, *, scale, lam_init, vd):
    pg = pl.program_id(1)
    nj = m_sc.shape[1]

    @pl.when(pg == 0)
    def _():
        m_sc[...] = jnp.full(m_sc.shape, -jnp.inf, F32)
        l_sc[...] = jnp.zeros(l_sc.shape, F32)
        acc_sc[...] = jnp.zeros(acc_sc.shape, F32)

    def update(kt, vt, valid_rows):
        s = jnp.dot(kt.astype(BF16), qbd_ref[...], preferred_element_type=F32) * scale
        if valid_rows is not None:
            rr = lax.broadcasted_iota(jnp.int32, s.shape, 0)
            s = jnp.where(rr < valid_rows, s, -jnp.inf)
        m_old = m_sc[...]
        m_new = jnp.maximum(m_old, jnp.max(s, axis=0, keepdims=True))
        alpha = jnp.exp(m_old - m_new)
        p = jnp.exp(s - m_new)
        l_sc[...] = alpha * l_sc[...] + jnp.sum(p, axis=0, keepdims=True)
        m_sc[...] = m_new
        for j in range(nj):
            h = j // 2
            contrib = jnp.sum(p[:, j:j + 1] * vt[:, h * vd:(h + 1) * vd], axis=0, keepdims=True)
            acc_sc[j:j + 1, :] = alpha[:, j:j + 1] * acc_sc[j:j + 1, :] + contrib

    update(kc_ref[...], vc_ref[...], None)

    @pl.when(pg == pl.num_programs(1) - 1)
    def _():
        update(kn_ref[...], vn_ref[...], 1)
        lam = _lambda(lq1_ref, lk1_ref, lq2_ref, lk2_ref, lam_init)
        l = l_sc[...]
        for h in range(nj // 2):
            o1 = acc_sc[2 * h:2 * h + 1, :] / l[:, 2 * h:2 * h + 1]
            o2 = acc_sc[2 * h + 1:2 * h + 2, :] / l[:, 2 * h + 1:2 * h + 2]
            o_ref[:, h * vd:(h + 1) * vd] = _diff_finish(o1, o2, lam, sw_ref[...], lam_init)


def diff_attention_decode(q, k_new, v_new, cache_k, cache_v, layer, page_table, lam_prm, subln_w, *, lam_init):
    r, width = q.shape
    hd = lam_prm[0].shape[-1]
    vd = 2 * hd
    nj = width // hd
    n_pages = page_table.shape[1]
    page = cache_k.shape[2]
    qbd = q.astype(F32).reshape(r, nj, hd)[:, :, :, None] * jnp.eye(nj, dtype=F32)[None, :, None, :]
    qbd = qbd.reshape(r, width, nj).astype(BF16)
    pad_rows = lambda x: jnp.pad(x[:, None, :], ((0, 0), (0, SUBLANES - 1), (0, 0)))
    vec = lambda n: pl.BlockSpec((1, n), lambda b, p, pt: (0, 0))
    grid_spec = pltpu.PrefetchScalarGridSpec(
        num_scalar_prefetch=1,
        grid=(r, n_pages),
        in_specs=[pl.BlockSpec((None, width, nj), lambda b, p, pt: (b, 0, 0)),
                  pl.BlockSpec((None, None, page, width), lambda b, p, pt: (layer, pt[b * n_pages + p], 0, 0)),
                  pl.BlockSpec((None, None, page, width), lambda b, p, pt: (layer, pt[b * n_pages + p], 0, 0)),
                  pl.BlockSpec((None, SUBLANES, width), lambda b, p, pt: (b, 0, 0)),
                  pl.BlockSpec((None, SUBLANES, width), lambda b, p, pt: (b, 0, 0)),
                  vec(hd), vec(hd), vec(hd), vec(hd), vec(vd)],
        out_specs=pl.BlockSpec((None, 1, width), lambda b, p, pt: (b, 0, 0)),
        scratch_shapes=[pltpu.VMEM((1, nj), F32), pltpu.VMEM((1, nj), F32), pltpu.VMEM((nj, vd), F32)],
    )
    out = pl.pallas_call(
        functools.partial(_dec_attn_kernel, scale=hd ** -0.5, lam_init=lam_init, vd=vd),
        grid_spec=grid_spec,
        out_shape=jax.ShapeDtypeStruct((r, 1, width), F32),
        compiler_params=_params("parallel", "arbitrary"),
        name="diff_attention_decode",
    )(page_table.reshape(-1), qbd, cache_k, cache_v, pad_rows(k_new), pad_rows(v_new),
      *[p.reshape(1, hd) for p in lam_prm], subln_w.reshape(1, vd))
    return out.reshape(r, width)


def _layer_weights(l, w_in, w_lift, w_out, ffn_w_gate, ffn_w_up, ffn_w_down, pool_w, glu_w, pool_width, attn_width,
                   ff_pad):
    d_ff = ffn_w_gate.shape[-1]
    padc = lambda w: jnp.pad(w, ((0, 0), (0, ff_pad - d_ff)))
    return dict(
        w_in=w_in[l].astype(BF16),
        wl_pool=w_lift[l, :pool_width].astype(BF16),
        wl_attn=w_lift[l, pool_width:pool_width + attn_width].astype(BF16),
        wl_ssm=w_lift[l, pool_width + attn_width:].astype(BF16),
        w_out=w_out[l].astype(BF16),
        w_gate=padc(ffn_w_gate[l].astype(BF16)),
        w_up=padc(ffn_w_up[l].astype(BF16)),
        w_down=jnp.pad(ffn_w_down[l].astype(BF16), ((0, ff_pad - d_ff), (0, 0))),
        pool_w=pool_w[l].astype(BF16),
        glu_w=glu_w[l].astype(BF16),
    )


def kernel(x_prompt, x_sample, cache_k, cache_v, state_pool, state_ssm_re, state_ssm_im, page_table, norm_mix_pre, norm_mix_post, norm_ffn_pre, norm_ffn_post, w_in, pool_w, pool_scale, lambda_q1, lambda_k1, lambda_q2, lambda_k2, subln_w, ssm_a_re, ssm_a_im, ssm_b_re, ssm_b_im, ssm_c_re, ssm_c_im, ssm_d, ssm_log_dt, glu_w, glu_b, w_lift, w_out, ffn_w_gate, ffn_w_up, ffn_w_down):
    bp, seq, d_model = x_prompt.shape
    db, dseq, _ = x_sample.shape
    assert dseq == 1
    depth = w_in.shape[0]
    page = cache_k.shape[2]
    n_past = page_table.shape[1] * page
    n_heads2, hd = cache_k.shape[3], cache_k.shape[4]
    attn_qw = n_heads2 * hd
    attn_vw = cache_v.shape[3] * cache_v.shape[4]
    pool_width = pool_w.shape[1] * pool_w.shape[2]
    gw = pool_w.shape[2]
    ssm_w = ssm_d.shape[1]
    n_groups, n_st = ssm_a_re.shape[1], ssm_a_re.shape[2]
    n_state = n_groups * n_st
    d_ff = ffn_w_gate.shape[-1]
    ff_pad = -(-d_ff // 1024) * 1024
    q_col0 = pool_width
    u_col0 = pool_width + 2 * attn_qw + attn_vw
    gate_col0 = u_col0 + ssm_w
    tn_in = 1024
    u_blk = u_col0 // tn_in
    assert SUBLANES % bp == 0 and attn_qw == attn_vw and ssm_w == tn_in and u_col0 % tn_in == 0

    mp = bp * seq
    rs = BF16_ROWS
    xp = x_prompt.reshape(mp, d_model)
    xs = jnp.pad(x_sample.reshape(db, d_model), ((0, rs - db), (0, 0)))
    pad_s = lambda a: jnp.pad(a, ((0, rs - db),) + ((0, 0),) * (a.ndim - 1))

    cos_p, sin_p = _rope_tables(jnp.arange(seq, dtype=jnp.int32), hd)
    cos_s, sin_s = _rope_tables(jnp.full((rs,), n_past, jnp.int32), hd)

    hp = rmsnorm_bf16(xp, norm_mix_pre[0])
    hs = rmsnorm_bf16(xs, norm_mix_pre[0])
    outs = {k: [] for k in ("kp", "vp", "pp", "rp", "ip", "ks", "vs", "ps", "rs", "is")}
    t_seq = _tile(seq, 1024)
    n_seq_blk = seq // t_seq

    for l in range(depth):
        lam_init = 0.8 - 0.6 * math.exp(-0.3 * l)
        w = _layer_weights(l, w_in, w_lift, w_out, ffn_w_gate, ffn_w_up, ffn_w_down, pool_w, glu_w,
                           pool_width, attn_vw, ff_pad)
        a_re, a_im, b_re, b_im, c_re, c_im = _s5_discretize(
            ssm_a_re[l], ssm_a_im[l], ssm_log_dt[l], ssm_b_re[l], ssm_b_im[l], ssm_c_re[l], ssm_c_im[l])
        s5prm = (a_re, a_im, b_re, b_im, c_re, c_im, ssm_d[l].reshape(1, ssm_w), w["glu_w"],
                 glu_b[l].reshape(1, ssm_w))
        lam_prm = (lambda_q1[l], lambda_k1[l], lambda_q2[l], lambda_k2[l])
        w_next = norm_mix_pre[l + 1] if l + 1 < depth else None

        n_in = w["w_in"].shape[1] // tn_in
        skip_u = lambda j: j + (j >= u_blk)
        z = matmul(hp, w["w_in"], tm=t_seq, tn=tn_in, n_blocks=n_in - 1, w_col=skip_u)
        u_tb = matmul(hp, w["w_in"], tm=t_seq, tn=tn_in, n_blocks=1, w_col=lambda j: u_blk,
                      out_shape=(seq, bp * ssm_w), out_map=lambda i, j: (i % n_seq_blk, i // n_seq_blk))
        q, k, kf, v, vf = rope_split(z, cos_p, sin_p, q_col0=q_col0, width=attn_qw, head_dim=hd, tm=512)
        o_pool, pbuf = pool_mixer_prompt(z.reshape(bp, seq, -1), w["pool_w"], pool_scale[l])
        o_attn = diff_attention_prompt(q, k, v, lam_prm, subln_w[l], batch=bp, lam_init=lam_init, t=ATTN_TILE)
        zero_h = jnp.zeros((SUBLANES, n_state), F32)
        o_ssm, hre, him = s5_mixer_prompt(u_tb.reshape(seq * bp, ssm_w), zero_h, zero_h, s5prm, bp, t_chunk=S5_TIME_CHUNK)
        t_lift = _tile(seq, 512)
        n_lift = seq // t_lift
        merged = lift_merge(o_pool.reshape(mp, pool_width), o_attn, o_ssm.reshape(seq, bp * ssm_w),
                            lambda i: (i % n_lift, i // n_lift),
                            w["wl_pool"], w["wl_attn"], w["wl_ssm"], z, gate_col0, tm=t_lift, tn=1024)
        y = matmul(merged, w["w_out"], tm=1024, tn=1024)
        xp, h2 = residual_norm(xp, y, norm_mix_post[l], norm_ffn_pre[l])
        act = swiglu(h2, w["w_gate"], w["w_up"], tm=1024, tn=512)
        f = matmul_ksplit(act, w["w_down"], tm=1024, tn=1024, tk=ff_pad // 4)
        xp, hp = residual_norm(xp, f, norm_ffn_post[l], w_next)
        outs["kp"].append(kf.reshape(bp, seq, n_heads2, hd))
        outs["vp"].append(vf.reshape(bp, seq, cache_v.shape[3], cache_v.shape[4]))
        outs["pp"].append(pbuf)
        outs["rp"].append(hre[SUBLANES - bp:].reshape(bp, n_groups, n_st))
        outs["ip"].append(him[SUBLANES - bp:].reshape(bp, n_groups, n_st))

        zs = matmul(hs, w["w_in"], tm=rs, tn=tn_in)
        qs, _, kfs, _, vfs = rope_split(zs, cos_s, sin_s, q_col0=q_col0, width=attn_qw, head_dim=hd, tm=rs)
        buf_t = pad_s(state_pool[l]).transpose(1, 0, 2)
        o_pool_s, nbuf_t, o_ssm_s, sre, sim = decode_mixers(
            zs, u_col0, buf_t, w["pool_w"], pool_scale[l],
            pad_s(state_ssm_re[l].reshape(db, n_state)), pad_s(state_ssm_im[l].reshape(db, n_state)),
            s5prm, n_past)
        o_attn_s = diff_attention_decode(
            qs[:db], kfs[:db], vfs[:db],
            cache_k.reshape(cache_k.shape[:3] + (attn_qw,)), cache_v.reshape(cache_v.shape[:3] + (attn_vw,)),
            l, page_table, lam_prm, subln_w[l], lam_init=lam_init)
        merged_s = lift_merge(o_pool_s.astype(BF16), pad_s(o_attn_s).astype(BF16), o_ssm_s.astype(BF16),
                              lambda i: (i, 0), w["wl_pool"], w["wl_attn"], w["wl_ssm"], zs, gate_col0,
                              tm=rs, tn=1024)
        ys = matmul(merged_s, w["w_out"], tm=rs, tn=1024)
        xs, h2s = residual_norm(xs, ys, norm_mix_post[l], norm_ffn_pre[l])
        act_s = swiglu(h2s, w["w_gate"], w["w_up"], tm=rs, tn=512)
        fs = matmul_ksplit(act_s, w["w_down"], tm=rs, tn=1024, tk=ff_pad // 4)
        xs, hs = residual_norm(xs, fs, norm_ffn_post[l], w_next)
        outs["ks"].append(kfs[:db].reshape(db, 1, n_heads2, hd))
        outs["vs"].append(vfs[:db].reshape(db, 1, cache_v.shape[3], cache_v.shape[4]))
        outs["ps"].append(nbuf_t.transpose(1, 0, 2)[:db])
        outs["rs"].append(sre[:db].reshape(db, n_groups, n_st))
        outs["is"].append(sim[:db].reshape(db, n_groups, n_st))

    st = lambda key: jnp.stack(outs[key])
    return (xp.reshape(bp, seq, d_model), xs[:db].reshape(db, 1, d_model),
            st("kp"), st("vp"), st("pp"), st("rp"), st("ip"),
            st("ks"), st("vs"), st("ps"), st("rs"), st("is"))
```

```python
import functools
import math

import jax
import jax.numpy as jnp
from jax import lax
from jax.experimental import pallas as pl
from jax.experimental.pallas import tpu as pltpu

F32 = jnp.float32
BF16 = jnp.bfloat16

V7X_VMEM_BYTES = 64 * 1024 * 1024
VMEM_LIMIT_BYTES = V7X_VMEM_BYTES - 8 * 1024 * 1024
SUBLANES = 8
LANES = 128
BF16_ROWS = 16

NORM_EPS = 1e-6
SUBLN_EPS = 1e-5
ROPE_THETA = 10000.0
POOL_WINDOWS = (2, 4, 8, 16)
POOL_BUF = max(POOL_WINDOWS) - 1
SSM_BLOCKS = 4
ATTN_TILE = 512
S5_TIME_CHUNK = 64


def _params(*sem):
    return pltpu.CompilerParams(dimension_semantics=sem, vmem_limit_bytes=VMEM_LIMIT_BYTES)


def _tile(n, pref):
    t = min(n, pref)
    while n % t:
        t //= 2
    return t


def _rms(x, w, eps):
    return x * lax.rsqrt(jnp.mean(x * x, axis=-1, keepdims=True) + eps) * w


def _rmsnorm_kernel(x_ref, w_ref, o_ref):
    o_ref[...] = _rms(x_ref[...], w_ref[...], NORM_EPS).astype(o_ref.dtype)


def rmsnorm_bf16(x, w):
    m, d = x.shape
    tm = _tile(m, 256)
    return pl.pallas_call(
        _rmsnorm_kernel,
        grid=(m // tm,),
        in_specs=[pl.BlockSpec((tm, d), lambda i: (i, 0)), pl.BlockSpec((1, d), lambda i: (0, 0))],
        out_specs=pl.BlockSpec((tm, d), lambda i: (i, 0)),
        out_shape=jax.ShapeDtypeStruct((m, d), BF16),
        compiler_params=_params("parallel"),
        name="rmsnorm",
    )(x, w.reshape(1, d))


def _resnorm_kernel(x_ref, y_ref, wpost_ref, wnext_ref, xo_ref, ho_ref):
    x = x_ref[...] + _rms(y_ref[...], wpost_ref[...], NORM_EPS)
    xo_ref[...] = x
    ho_ref[...] = _rms(x, wnext_ref[...], NORM_EPS).astype(ho_ref.dtype)


def _resnorm_last_kernel(x_ref, y_ref, wpost_ref, xo_ref):
    xo_ref[...] = x_ref[...] + _rms(y_ref[...], wpost_ref[...], NORM_EPS)


def residual_norm(x, y, w_post, w_next):
    m, d = x.shape
    tm = _tile(m, 256)
    row = pl.BlockSpec((tm, d), lambda i: (i, 0))
    vec = pl.BlockSpec((1, d), lambda i: (0, 0))
    if w_next is None:
        return pl.pallas_call(
            _resnorm_last_kernel, grid=(m // tm,), in_specs=[row, row, vec], out_specs=row,
            out_shape=jax.ShapeDtypeStruct((m, d), F32), compiler_params=_params("parallel"),
            name="resnorm_last",
        )(x, y, w_post.reshape(1, d)), None
    return pl.pallas_call(
        _resnorm_kernel, grid=(m // tm,), in_specs=[row, row, vec, vec], out_specs=[row, row],
        out_shape=[jax.ShapeDtypeStruct((m, d), F32), jax.ShapeDtypeStruct((m, d), BF16)],
        compiler_params=_params("parallel"), name="resnorm",
    )(x, y, w_post.reshape(1, d), w_next.reshape(1, d))


def _first_row_step():
    return pl.program_id(1) == 0


def _proj_kernel(xs_ref, xp_ref, w_ref, os_ref, op_ref, wbf_ref):
    @pl.when(_first_row_step())
    def _():
        wbf_ref[...] = w_ref[...].astype(BF16)
        os_ref[...] = jnp.dot(xs_ref[...], wbf_ref[...], preferred_element_type=F32).astype(os_ref.dtype)

    op_ref[...] = jnp.dot(xp_ref[...], wbf_ref[...], preferred_element_type=F32).astype(op_ref.dtype)


def project(xs, xp, w, layer, *, tm, tn, n_blocks, w_col, s_shape, s_map, p_shape, p_map, out_dtype=F32):
    rs, k = xs.shape
    tm = _tile(xp.shape[0], tm)
    return pl.pallas_call(
        _proj_kernel,
        grid=(n_blocks, xp.shape[0] // tm),
        in_specs=[pl.BlockSpec((rs, k), lambda j, i: (0, 0)),
                  pl.BlockSpec((tm, k), lambda j, i: (i, 0)),
                  pl.BlockSpec((None, k, tn), lambda j, i: (layer, 0, w_col(j)))],
        out_specs=[pl.BlockSpec((rs, tn), lambda j, i: s_map(j)),
                   pl.BlockSpec((tm, tn), p_map)],
        out_shape=[jax.ShapeDtypeStruct(s_shape, out_dtype), jax.ShapeDtypeStruct(p_shape, out_dtype)],
        scratch_shapes=[pltpu.VMEM((k, tn), BF16)],
        compiler_params=_params("parallel", "arbitrary"),
        name="project",
    )(xs, xp, w)


def _swiglu(x, wg_ref, wu_ref):
    g = jnp.dot(x, wg_ref[...], preferred_element_type=F32)
    u = jnp.dot(x, wu_ref[...], preferred_element_type=F32)
    return g * jax.nn.sigmoid(g) * u


def _swiglu_kernel(xs_ref, xp_ref, wg_ref, wu_ref, os_ref, op_ref, wgbf_ref, wubf_ref):
    @pl.when(_first_row_step())
    def _():
        wgbf_ref[...] = wg_ref[...].astype(BF16)
        wubf_ref[...] = wu_ref[...].astype(BF16)
        os_ref[...] = _swiglu(xs_ref[...], wgbf_ref, wubf_ref).astype(os_ref.dtype)

    op_ref[...] = _swiglu(xp_ref[...], wgbf_ref, wubf_ref).astype(op_ref.dtype)


def swiglu(xs, xp, wg, wu, layer, *, tm, tn):
    rs, k = xs.shape
    m = xp.shape[0]
    n = wg.shape[2]
    tm = _tile(m, tm)
    wspec = pl.BlockSpec((None, k, tn), lambda j, i: (layer, 0, j))
    return pl.pallas_call(
        _swiglu_kernel,
        grid=(n // tn, m // tm),
        in_specs=[pl.BlockSpec((rs, k), lambda j, i: (0, 0)),
                  pl.BlockSpec((tm, k), lambda j, i: (i, 0)), wspec, wspec],
        out_specs=[pl.BlockSpec((rs, tn), lambda j, i: (0, j)),
                   pl.BlockSpec((tm, tn), lambda j, i: (i, j))],
        out_shape=[jax.ShapeDtypeStruct((rs, n), BF16), jax.ShapeDtypeStruct((m, n), BF16)],
        scratch_shapes=[pltpu.VMEM((k, tn), BF16), pltpu.VMEM((k, tn), BF16)],
        compiler_params=_params("parallel", "arbitrary"),
        name="swiglu",
    )(xs, xp, wg, wu)


def _merge(o_pool_ref, o_attn_ref, o_ssm_ref, g0_ref, g1_ref, g2_ref, wbf_ref):
    pw, aw = o_pool_ref.shape[1], o_attn_ref.shape[1]
    up = jnp.dot(o_pool_ref[...], wbf_ref[0:pw, :], preferred_element_type=F32)
    ua = jnp.dot(o_attn_ref[...], wbf_ref[pw:pw + aw, :], preferred_element_type=F32)
    us = jnp.dot(o_ssm_ref[...], wbf_ref[pw + aw:, :], preferred_element_type=F32)
    return (jax.nn.sigmoid(g0_ref[...]) * up + jax.nn.sigmoid(g1_ref[...]) * ua
            + jax.nn.sigmoid(g2_ref[...]) * us)


def _lift_kernel(*refs):
    s_in, p_in, (w_ref, os_ref, op_ref, wbf_ref) = refs[0:6], refs[6:12], refs[12:]

    @pl.when(_first_row_step())
    def _():
        wbf_ref[...] = w_ref[...].astype(BF16)
        os_ref[...] = _merge(*s_in, wbf_ref).astype(os_ref.dtype)

    op_ref[...] = _merge(*p_in, wbf_ref).astype(op_ref.dtype)


def lift_merge(branches_s, zs, branches_p, zp, ssm_map_p, w_lift, layer, gate_col0, *, tm, tn):
    rs = zs.shape[0]
    m = zp.shape[0]
    k, d = w_lift.shape[1], w_lift.shape[2]
    tm = _tile(m, tm)
    gb, nb = gate_col0 // tn, d // tn
    widths = [o.shape[1] for o in branches_s]
    s_specs = [pl.BlockSpec((rs, wd), lambda j, i: (0, 0)) for wd in widths]
    s_gates = [pl.BlockSpec((rs, tn), lambda j, i, br=br: (0, gb + br * nb + j)) for br in range(3)]
    p_specs = [pl.BlockSpec((tm, widths[0]), lambda j, i: (i, 0)),
               pl.BlockSpec((tm, widths[1]), lambda j, i: (i, 0)),
               pl.BlockSpec((tm, widths[2]), lambda j, i: ssm_map_p(i))]
    p_gates = [pl.BlockSpec((tm, tn), lambda j, i, br=br: (i, gb + br * nb + j)) for br in range(3)]
    return pl.pallas_call(
        _lift_kernel,
        grid=(nb, m // tm),
        in_specs=s_specs + s_gates + p_specs + p_gates
        + [pl.BlockSpec((None, k, tn), lambda j, i: (layer, 0, j))],
        out_specs=[pl.BlockSpec((rs, tn), lambda j, i: (0, j)), pl.BlockSpec((tm, tn), lambda j, i: (i, j))],
        out_shape=[jax.ShapeDtypeStruct((rs, d), BF16), jax.ShapeDtypeStruct((m, d), BF16)],
        scratch_shapes=[pltpu.VMEM((k, tn), BF16)],
        compiler_params=_params("parallel", "arbitrary"),
        name="lift_merge",
    )(*branches_s, zs, zs, zs, *branches_p, zp, zp, zp, w_lift)


def _mm_acc_kernel(x_ref, w_ref, o_ref, acc_ref):
    kk = pl.program_id(2)

    @pl.when(kk == 0)
    def _():
        acc_ref[...] = jnp.zeros_like(acc_ref)

    acc_ref[...] += jnp.dot(x_ref[...], w_ref[...], preferred_element_type=F32)

    @pl.when(kk == pl.num_programs(2) - 1)
    def _():
        o_ref[...] = acc_ref[...]


def matmul_ksplit(x, w, *, tm, tn, tk):
    m, k = x.shape
    n = w.shape[1]
    tm = _tile(m, tm)
    return pl.pallas_call(
        _mm_acc_kernel,
        grid=(m // tm, n // tn, k // tk),
        in_specs=[pl.BlockSpec((tm, tk), lambda i, j, kk: (i, kk)),
                  pl.BlockSpec((tk, tn), lambda i, j, kk: (kk, j))],
        out_specs=pl.BlockSpec((tm, tn), lambda i, j, kk: (i, j)),
        out_shape=jax.ShapeDtypeStruct((m, n), F32),
        scratch_shapes=[pltpu.VMEM((tm, tn), F32)],
        compiler_params=_params("parallel", "parallel", "arbitrary"),
        name="matmul_ksplit",
    )(x, w)


def _rope_tables(pos, head_dim):
    half = head_dim // 2
    inv = ROPE_THETA ** (-jnp.arange(half, dtype=F32) / half)
    ang = pos.astype(F32)[:, None] * inv[None, :]
    cos, sin = jnp.cos(ang), jnp.sin(ang)
    return jnp.concatenate([cos, cos], axis=-1), jnp.concatenate([-sin, sin], axis=-1)


def _rope_kernel(q_ref, k_ref, v_ref, cos_ref, sin_ref, qo_ref, ko_ref, kf_ref, vo_ref, vf_ref, *, head_dim):
    cos, sin = cos_ref[...], sin_ref[...]
    for h in range(q_ref.shape[1] // head_dim):
        cols = slice(h * head_dim, (h + 1) * head_dim)
        for src, outs in ((q_ref, (qo_ref,)), (k_ref, (ko_ref, kf_ref))):
            x = src[:, cols]
            r = x * cos + pltpu.roll(x, head_dim // 2, 1) * sin
            for o in outs:
                o[:, cols] = r.astype(o.dtype)
    v = v_ref[...]
    vo_ref[...] = v.astype(vo_ref.dtype)
    vf_ref[...] = v


def rope_split(z, cos, sin, *, q_col0, width, head_dim, tm):
    m = z.shape[0]
    s_tab = cos.shape[0]
    tm = _tile(min(m, s_tab), tm)
    tc = 1024
    nc = width // tc
    qb = q_col0 // tc
    n_tab = s_tab // tm

    def zspec(off):
        return pl.BlockSpec((tm, tc), lambda i, j: (i, off + j))

    tab = pl.BlockSpec((tm, head_dim), lambda i, j: (i % n_tab, 0))
    out = pl.BlockSpec((tm, tc), lambda i, j: (i, j))
    return pl.pallas_call(
        functools.partial(_rope_kernel, head_dim=head_dim),
        grid=(m // tm, nc),
        in_specs=[zspec(qb), zspec(qb + nc), zspec(qb + 2 * nc), tab, tab],
        out_specs=[out] * 5,
        out_shape=[jax.ShapeDtypeStruct((m, width), dt) for dt in (BF16, BF16, F32, BF16, F32)],
        compiler_params=_params("parallel", "arbitrary"),
        name="rope_split",
    )(z, z, z, cos, sin)


def _pool_features(ext_load, p, pos0, gw):
    s = p.shape[0]
    pos = pos0 + lax.broadcasted_iota(jnp.int32, (s, 1), 0)
    feats = []
    for g, w in enumerate(POOL_WINDOWS):
        cols = slice(g * gw, (g + 1) * gw)
        win = p[:, cols]
        for j in range(1, w):
            win = win + ext_load(j, cols)
        cnt = jnp.minimum(pos + 1, w).astype(F32)
        feats.append(win / cnt - p[:, cols])
    return feats


def _pool_kernel(p_ref, w_ref, scale_ref, o_ref, buf_ref, ext_ref, *, s, gw):
    pad = BF16_ROWS
    ext_ref[0:pad, :] = jnp.zeros((pad, ext_ref.shape[1]), F32)
    ext_ref[pad:pad + s, :] = p_ref[...]
    p = p_ref[...]
    feats = _pool_features(lambda j, cols: ext_ref[pad - j:pad - j + s, cols], p, 0, gw)
    for g, f in enumerate(feats):
        cols = slice(g * gw, (g + 1) * gw)
        o = jnp.dot(f.astype(BF16), w_ref[g], preferred_element_type=F32) * scale_ref[:, cols]
        o_ref[:, cols] = o.astype(o_ref.dtype)
    buf_ref[...] = ext_ref[pad + s - POOL_BUF:pad + s, :]


def pool_mixer_prompt(z3, pool_w, pool_scale):
    b, s, _ = z3.shape
    ng, gw, _ = pool_w.shape
    width = ng * gw
    return pl.pallas_call(
        functools.partial(_pool_kernel, s=s, gw=gw),
        grid=(b,),
        in_specs=[pl.BlockSpec((None, s, width), lambda i: (i, 0, 0)),
                  pl.BlockSpec((ng, gw, gw), lambda i: (0, 0, 0)),
                  pl.BlockSpec((1, width), lambda i: (0, 0))],
        out_specs=[pl.BlockSpec((None, s, width), lambda i: (i, 0, 0)),
                   pl.BlockSpec((None, POOL_BUF, width), lambda i: (i, 0, 0))],
        out_shape=[jax.ShapeDtypeStruct((b, s, width), BF16),
                   jax.ShapeDtypeStruct((b, POOL_BUF, width), F32)],
        scratch_shapes=[pltpu.VMEM((BF16_ROWS + s, width), F32)],
        compiler_params=_params("parallel"),
        name="pool_mixer",
    )(z3, pool_w, pool_scale.reshape(1, width))


def _s5_discretize(a_re, a_im, log_dt, b_re, b_im, c_re, c_im):
    g, p = a_re.shape
    c = b_re.shape[-1]
    a = lax.complex(a_re.astype(F32), a_im.astype(F32))
    dt = jnp.exp(log_dt.astype(F32))[:, None]
    a_bar = jnp.exp(a * dt)
    b_bar = ((a_bar - 1.0) / a)[..., None] * lax.complex(b_re.astype(F32), b_im.astype(F32))
    gb = g // SSM_BLOCKS
    eye = jnp.eye(gb, dtype=F32)

    def b_blocks(x):
        x = x.reshape(SSM_BLOCKS, gb, p, c)
        return jnp.einsum('kgpc,gh->kgchp', x, eye).reshape(SSM_BLOCKS, gb * c, gb * p).astype(BF16)

    def c_blocks(x):
        x = x.reshape(SSM_BLOCKS, gb, c, p)
        return jnp.einsum('kgcp,gh->kgphc', x, eye).reshape(SSM_BLOCKS, gb * p, gb * c).astype(BF16)

    return (jnp.real(a_bar).reshape(1, g * p), jnp.imag(a_bar).reshape(1, g * p),
            b_blocks(jnp.real(b_bar)), b_blocks(jnp.imag(b_bar)),
            c_blocks(c_re.astype(F32)), c_blocks(-c_im.astype(F32)))


def _s5_input_proj(u_bf16, br_ref, bi_ref, store_re, store_im):
    cw, sw = br_ref.shape[1], br_ref.shape[2]
    for k in range(SSM_BLOCKS):
        uk = u_bf16[:, k * cw:(k + 1) * cw]
        store_re(k * sw, sw, jnp.dot(uk, br_ref[k], preferred_element_type=F32))
        store_im(k * sw, sw, jnp.dot(uk, bi_ref[k], preferred_element_type=F32))


def _s5_output(hs_re, hs_im, u, cr_ref, ci_ref, d_ref, gw_ref, gb_ref):
    sw, cw = cr_ref.shape[1], cr_ref.shape[2]
    ys = []
    for k in range(SSM_BLOCKS):
        ys.append(jnp.dot(hs_re(k * sw, sw).astype(BF16), cr_ref[k], preferred_element_type=F32)
                  + jnp.dot(hs_im(k * sw, sw).astype(BF16), ci_ref[k], preferred_element_type=F32))
    y = jnp.concatenate(ys, axis=-1) + d_ref[...] * u
    zz = jax.nn.gelu(y, approximate=True)
    gate = jnp.dot(zz.astype(BF16), gw_ref[...], preferred_element_type=F32) + gb_ref[...]
    return zz * jax.nn.sigmoid(gate)


def _s5_kernel(u_ref, h0r_ref, h0i_ref, ar_ref, ai_ref, br_ref, bi_ref, cr_ref, ci_ref, d_ref, gw_ref, gb_ref,
               o_ref, hr_ref, hi_ref, xr_sc, xi_sc, *, nb, lane_chunk):
    rows, n_state = xr_sc.shape
    steps = SUBLANES // nb

    @pl.when(pl.program_id(0) == 0)
    def _():
        hr_ref[...] = h0r_ref[...]
        hi_ref[...] = h0i_ref[...]

    u = u_ref[...]

    def store(sc):
        def f(c0, w, val):
            sc[:, c0:c0 + w] = val
        return f

    _s5_input_proj(u.astype(BF16), br_ref, bi_ref, store(xr_sc), store(xi_sc))

    row = lax.broadcasted_iota(jnp.int32, (SUBLANES, lane_chunk), 0)
    for c0 in range(0, n_state, lane_chunk):
        cols = slice(c0, c0 + lane_chunk)
        ar = jnp.broadcast_to(ar_ref[:, cols], (SUBLANES, lane_chunk))
        ai = jnp.broadcast_to(ai_ref[:, cols], (SUBLANES, lane_chunk))

        def tile(i, carry, cols=cols, ar=ar, ai=ai):
            hr, hi = carry
            r0 = pl.multiple_of(i * SUBLANES, SUBLANES)
            xr = xr_sc[pl.ds(r0, SUBLANES), cols]
            xi = xi_sc[pl.ds(r0, SUBLANES), cols]
            out_r, out_i = xr, xi
            for k in range(steps):
                pr = pltpu.roll(hr, nb, 0) if nb < SUBLANES else hr
                pi = pltpu.roll(hi, nb, 0) if nb < SUBLANES else hi
                hr = ar * pr - ai * pi + xr
                hi = ar * pi + ai * pr + xi
                sel = (row >= k * nb) & (row < (k + 1) * nb)
                out_r = jnp.where(sel, hr, out_r)
                out_i = jnp.where(sel, hi, out_i)
            xr_sc[pl.ds(r0, SUBLANES), cols] = out_r
            xi_sc[pl.ds(r0, SUBLANES), cols] = out_i
            return hr, hi

        hr, hi = lax.fori_loop(0, rows // SUBLANES, tile, (hr_ref[:, cols], hi_ref[:, cols]))
        hr_ref[:, cols] = hr
        hi_ref[:, cols] = hi

    out = _s5_output(lambda c0, w: xr_sc[:, c0:c0 + w], lambda c0, w: xi_sc[:, c0:c0 + w],
                     u, cr_ref, ci_ref, d_ref, gw_ref, gb_ref)
    o_ref[...] = out.astype(o_ref.dtype)


def s5_mixer_prompt(u_tb, h0_re, h0_im, prm, nb, *, t_chunk):
    n_rows, cw = u_tb.shape
    a_re, a_im, b_re, b_im, c_re, c_im, d, glu_w, glu_b = prm
    n_state = a_re.shape[1]
    rows = t_chunk * nb
    const2 = lambda i: (0, 0)
    const3 = lambda i: (0, 0, 0)
    return pl.pallas_call(
        functools.partial(_s5_kernel, nb=nb, lane_chunk=512),
        grid=(n_rows // rows,),
        in_specs=[pl.BlockSpec((rows, cw), lambda i: (i, 0)),
                  pl.BlockSpec((SUBLANES, n_state), const2), pl.BlockSpec((SUBLANES, n_state), const2),
                  pl.BlockSpec((1, n_state), const2), pl.BlockSpec((1, n_state), const2),
                  pl.BlockSpec(b_re.shape, const3), pl.BlockSpec(b_im.shape, const3),
                  pl.BlockSpec(c_re.shape, const3), pl.BlockSpec(c_im.shape, const3),
                  pl.BlockSpec((1, cw), const2), pl.BlockSpec((cw, cw), const2), pl.BlockSpec((1, cw), const2)],
        out_specs=[pl.BlockSpec((rows, cw), lambda i: (i, 0)),
                   pl.BlockSpec((SUBLANES, n_state), const2), pl.BlockSpec((SUBLANES, n_state), const2)],
        out_shape=[jax.ShapeDtypeStruct((n_rows, cw), BF16),
                   jax.ShapeDtypeStruct((SUBLANES, n_state), F32),
                   jax.ShapeDtypeStruct((SUBLANES, n_state), F32)],
        scratch_shapes=[pltpu.VMEM((rows, n_state), F32), pltpu.VMEM((rows, n_state), F32)],
        compiler_params=_params("arbitrary"),
        name="s5_mixer",
    )(u_tb, h0_re, h0_im, a_re, a_im, b_re, b_im, c_re, c_im, d, glu_w, glu_b)


def _decode_mix_kernel(p_ref, u_ref, buf_ref, pw_ref, ps_ref, h0r_ref, h0i_ref, ar_ref, ai_ref,
                       br_ref, bi_ref, cr_ref, ci_ref, d_ref, gw_ref, gb_ref,
                       op_ref, nbuf_ref, os_ref, hr_ref, hi_ref, *, pos0, gw):
    p = p_ref[...]
    feats = []
    for g, w in enumerate(POOL_WINDOWS):
        cols = slice(g * gw, (g + 1) * gw)
        win = p[:, cols]
        for j in range(1, w):
            win = win + buf_ref[POOL_BUF - j, :, cols]
        feats.append(win / float(min(pos0 + 1, w)) - p[:, cols])
    for g, f in enumerate(feats):
        cols = slice(g * gw, (g + 1) * gw)
        o = jnp.dot(f.astype(BF16), pw_ref[g], preferred_element_type=F32) * ps_ref[:, cols]
        op_ref[:, cols] = o
    for r in range(POOL_BUF - 1):
        nbuf_ref[r] = buf_ref[r + 1]
    nbuf_ref[POOL_BUF - 1] = p

    u = u_ref[...]

    def store(ref):
        def f(c0, w, val):
            ref[:, c0:c0 + w] = val
        return f

    _s5_input_proj(u.astype(BF16), br_ref, bi_ref, store(hr_ref), store(hi_ref))
    h0r, h0i = h0r_ref[...], h0i_ref[...]
    ar, ai = ar_ref[...], ai_ref[...]
    hr = ar * h0r - ai * h0i + hr_ref[...]
    hi = ar * h0i + ai * h0r + hi_ref[...]
    hr_ref[...] = hr
    hi_ref[...] = hi
    os_ref[...] = _s5_output(lambda c0, w: hr_ref[:, c0:c0 + w], lambda c0, w: hi_ref[:, c0:c0 + w],
                             u, cr_ref, ci_ref, d_ref, gw_ref, gb_ref)


def decode_mixers(z, u, buf_t, pool_w, pool_scale, h0_re, h0_im, s5prm, pos0):
    r = z.shape[0]
    ng, gw, _ = pool_w.shape
    pw = ng * gw
    a_re, a_im, b_re, b_im, c_re, c_im, d, glu_w, glu_b = s5prm
    cw = d.shape[1]
    n_state = a_re.shape[1]
    full = lambda a: pl.BlockSpec(a.shape, lambda i: (0,) * a.ndim)
    args = (buf_t, pool_w, pool_scale.reshape(1, pw), h0_re, h0_im, a_re, a_im, b_re, b_im, c_re, c_im,
            d, glu_w, glu_b)
    return pl.pallas_call(
        functools.partial(_decode_mix_kernel, pos0=pos0, gw=gw),
        grid=(1,),
        in_specs=[pl.BlockSpec((r, pw), lambda i: (0, 0)),
                  pl.BlockSpec((r, cw), lambda i: (0, 0))] + [full(a) for a in args],
        out_specs=[pl.BlockSpec((r, pw), lambda i: (0, 0)),
                   pl.BlockSpec((POOL_BUF, r, pw), lambda i: (0, 0, 0)),
                   pl.BlockSpec((r, cw), lambda i: (0, 0)),
                   pl.BlockSpec((r, n_state), lambda i: (0, 0)),
                   pl.BlockSpec((r, n_state), lambda i: (0, 0))],
        out_shape=[jax.ShapeDtypeStruct((r, pw), F32),
                   jax.ShapeDtypeStruct((POOL_BUF, r, pw), F32),
                   jax.ShapeDtypeStruct((r, cw), F32),
                   jax.ShapeDtypeStruct((r, n_state), F32),
                   jax.ShapeDtypeStruct((r, n_state), F32)],
        compiler_params=_params("arbitrary"),
        name="decode_mixers",
    )(z, u, *args)


def _lambda(lq1_ref, lk1_ref, lq2_ref, lk2_ref, lam_init):
    return (jnp.exp(jnp.sum(lq1_ref[...] * lk1_ref[...], axis=-1, keepdims=True))
            - jnp.exp(jnp.sum(lq2_ref[...] * lk2_ref[...], axis=-1, keepdims=True)) + lam_init)


def _diff_finish(o1, o2, lam, sw, lam_init):
    o = o1 - lam * o2
    return _rms(o, sw, SUBLN_EPS) * (1.0 - lam_init)


def _attn_kernel(q_ref, k_ref, v_ref, lq1_ref, lk1_ref, lq2_ref, lk2_ref, sw_ref, o_ref,
                 m_sc, l_sc, acc_sc, *, t, hd, scale, lam_init):
    i = pl.program_id(2)
    q = q_ref[...]
    c_exp = scale * math.log2(math.e)
    m_sc[...] = jnp.full(m_sc.shape, -jnp.inf, F32)
    l_sc[...] = jnp.zeros(l_sc.shape, F32)
    acc_sc[...] = jnp.zeros(acc_sc.shape, F32)

    def kv_tile(j, diagonal):
        r0 = pl.multiple_of(j * t, t)
        kt = k_ref[pl.ds(r0, t), :]
        vt = v_ref[pl.ds(r0, t), :]
        for mp in range(2):
            s = lax.dot_general(q[:, mp * hd:(mp + 1) * hd], kt[:, mp * hd:(mp + 1) * hd],
                                (((1,), (1,)), ((), ())), preferred_element_type=F32)
            if diagonal:
                rr = lax.broadcasted_iota(jnp.int32, (t, t), 0)
                cc = lax.broadcasted_iota(jnp.int32, (t, t), 1)
                s = jnp.where(cc <= rr, s, -jnp.inf)
            m_old = m_sc[mp]
            m_new = jnp.maximum(m_old, jnp.max(s, axis=-1, keepdims=True))
            alpha = jnp.exp2((m_old - m_new) * c_exp)
            p = jnp.exp2((s - m_new) * c_exp)
            l_sc[mp] = alpha * l_sc[mp] + jnp.sum(p, axis=-1, keepdims=True)
            acc_sc[mp] = alpha * acc_sc[mp] + jnp.dot(p.astype(BF16), vt, preferred_element_type=F32)
            m_sc[mp] = m_new

    def body(j, c):
        kv_tile(j, False)
        return c

    lax.fori_loop(0, i, body, 0)
    kv_tile(i, True)
    lam = _lambda(lq1_ref, lk1_ref, lq2_ref, lk2_ref, lam_init)
    o = _diff_finish(acc_sc[0] / l_sc[0], acc_sc[1] / l_sc[1], lam, sw_ref[...], lam_init)
    o_ref[...] = o.astype(o_ref.dtype)


def diff_attention_prompt(q, k, v, lam_prm, subln_w, *, batch, lam_init, t):
    m, width = q.shape
    s = m // batch
    hd = lam_prm[0].shape[-1]
    vd = 2 * hd
    nh = width // vd
    t = _tile(s, t)
    nq = s // t
    vec = lambda n: pl.BlockSpec((1, n), lambda b, h, i: (0, 0))
    return pl.pallas_call(
        functools.partial(_attn_kernel, t=t, hd=hd, scale=hd ** -0.5, lam_init=lam_init),
        grid=(batch, nh, nq),
        in_specs=[pl.BlockSpec((t, vd), lambda b, h, i: (b * nq + i, h)),
                  pl.BlockSpec((s, vd), lambda b, h, i: (b, h)),
                  pl.BlockSpec((s, vd), lambda b, h, i: (b, h)),
                  vec(hd), vec(hd), vec(hd), vec(hd), vec(vd)],
        out_specs=pl.BlockSpec((t, vd), lambda b, h, i: (b * nq + i, h)),
        out_shape=jax.ShapeDtypeStruct((m, width), BF16),
        scratch_shapes=[pltpu.VMEM((2, t, 1), F32), pltpu.VMEM((2, t, 1), F32), pltpu.VMEM((2, t, vd), F32)],
        compiler_params=_params("parallel", "parallel", "arbitrary"),
        name="diff_attention",
    )(q, k, v, *[p.reshape(1, hd) for p in lam_prm], subln_w.reshape(1, vd))


def _dec_attn_kernel(pt_ref, q_ref, kc_ref, vc_ref, kn_ref, vn_ref, lq1_ref, lk1_ref, lq2_ref, lk2_ref,
                     sw_ref, o_ref, m_sc, l_sc, a_sc, p_sc, acc_sc, *, scale, lam_init):
    pg = pl.program_id(1)
    nh = acc_sc.shape[1]

    @pl.when(pg == 0)
    def _():
        m_sc[...] = jnp.full(m_sc.shape, -jnp.inf, F32)
        l_sc[...] = jnp.zeros(l_sc.shape, F32)
        acc_sc[...] = jnp.zeros(acc_sc.shape, F32)

    q = q_ref[...].astype(F32)

    def update(k3, v3):
        rows = k3.shape[0]
        s = jnp.sum(k3 * q[None], axis=-1, keepdims=True) * scale
        m_old = m_sc[...]
        m_new = jnp.maximum(m_old, jnp.max(s, axis=0, keepdims=True))
        a_sc[...] = jnp.exp(m_old - m_new)
        p = jnp.exp(s - m_new)
        l_sc[...] = a_sc[...] * l_sc[...] + jnp.sum(p, axis=0, keepdims=True)
        m_sc[...] = m_new
        p_sc[0:rows] = p
        for mp in range(2):
            pm = p_sc[0:rows, pl.ds(mp, nh, stride=2), :]
            am = a_sc[0, pl.ds(mp, nh, stride=2), :]
            acc_sc[mp] = am * acc_sc[mp] + jnp.sum(pm * v3, axis=0)

    update(kc_ref[...], vc_ref[...])

    @pl.when(pg == pl.num_programs(1) - 1)
    def _():
        update(kn_ref[...][None], vn_ref[...][None])
        lam = _lambda(lq1_ref, lk1_ref, lq2_ref, lk2_ref, lam_init)
        o1 = acc_sc[0] / l_sc[0, pl.ds(0, nh, stride=2), :]
        o2 = acc_sc[1] / l_sc[0, pl.ds(1, nh, stride=2), :]
        o_ref[...] = _diff_finish(o1, o2, lam, sw_ref[...], lam_init)


def diff_attention_decode(q, k_new, v_new, cache_k, cache_v, layer, page_table, lam_prm, subln_w, *, lam_init):
    r, nj, hd = q.shape
    nh, vd = v_new.shape[1], v_new.shape[2]
    n_pages = page_table.shape[1]
    page = cache_k.shape[2]
    vec = lambda n: pl.BlockSpec((1, n), lambda b, p, pt: (0, 0))
    page_map = lambda b, p, pt: (layer, pt[b * n_pages + p], 0, 0, 0)
    row_map = lambda b, p, pt: (b, 0, 0)
    grid_spec = pltpu.PrefetchScalarGridSpec(
        num_scalar_prefetch=1,
        grid=(r, n_pages),
        in_specs=[pl.BlockSpec((None, nj, hd), row_map),
                  pl.BlockSpec((None, None, page, nj, hd), page_map),
                  pl.BlockSpec((None, None, page, nh, vd), page_map),
                  pl.BlockSpec((None, nj, hd), row_map),
                  pl.BlockSpec((None, nh, vd), row_map),
                  vec(hd), vec(hd), vec(hd), vec(hd), vec(vd)],
        out_specs=pl.BlockSpec((None, nh, vd), row_map),
        scratch_shapes=[pltpu.VMEM((1, nj, 1), F32), pltpu.VMEM((1, nj, 1), F32), pltpu.VMEM((1, nj, 1), F32),
                        pltpu.VMEM((page, nj, 1), F32), pltpu.VMEM((2, nh, vd), F32)],
    )
    return pl.pallas_call(
        functools.partial(_dec_attn_kernel, scale=hd ** -0.5, lam_init=lam_init),
        grid_spec=grid_spec,
        out_shape=jax.ShapeDtypeStruct((r, nh, vd), F32),
        compiler_params=_params("parallel", "arbitrary"),
        name="diff_attention_decode",
    )(page_table.reshape(-1), q, cache_k, cache_v, k_new, v_new,
      *[p.reshape(1, hd) for p in lam_prm], subln_w.reshape(1, vd))


def kernel(x_prompt, x_sample, cache_k, cache_v, state_pool, state_ssm_re, state_ssm_im, page_table, norm_mix_pre, norm_mix_post, norm_ffn_pre, norm_ffn_post, w_in, pool_w, pool_scale, lambda_q1, lambda_k1, lambda_q2, lambda_k2, subln_w, ssm_a_re, ssm_a_im, ssm_b_re, ssm_b_im, ssm_c_re, ssm_c_im, ssm_d, ssm_log_dt, glu_w, glu_b, w_lift, w_out, ffn_w_gate, ffn_w_up, ffn_w_down):
    bp, seq, d_model = x_prompt.shape
    db, dseq, _ = x_sample.shape
    assert dseq == 1
    depth = w_in.shape[0]
    page = cache_k.shape[2]
    n_past = page_table.shape[1] * page
    n_heads2, hd = cache_k.shape[3], cache_k.shape[4]
    n_heads, vd = cache_v.shape[3], cache_v.shape[4]
    attn_qw = n_heads2 * hd
    attn_vw = n_heads * vd
    pool_width = pool_w.shape[1] * pool_w.shape[2]
    ssm_w = ssm_d.shape[1]
    n_groups, n_st = ssm_a_re.shape[1], ssm_a_re.shape[2]
    n_state = n_groups * n_st
    d_ff = ffn_w_gate.shape[-1]
    in_width = w_in.shape[2]
    q_col0 = pool_width
    u_col0 = pool_width + 2 * attn_qw + attn_vw
    gate_col0 = u_col0 + ssm_w
    tn = 512
    u_blk, u_nblk = u_col0 // tn, ssm_w // tn
    assert SUBLANES % bp == 0 and attn_qw == attn_vw and u_col0 % tn == 0 and ssm_w % tn == 0

    mp = bp * seq
    rs = BF16_ROWS
    xp = x_prompt.reshape(mp, d_model)
    xs = jnp.pad(x_sample.reshape(db, d_model), ((0, rs - db), (0, 0)))
    pad_s = lambda a: jnp.pad(a, ((0, rs - db),) + ((0, 0),) * (a.ndim - 1))

    cos_p, sin_p = _rope_tables(jnp.arange(seq, dtype=jnp.int32), hd)
    cos_s, sin_s = _rope_tables(jnp.full((rs,), n_past, jnp.int32), hd)

    hp = rmsnorm_bf16(xp, norm_mix_pre[0])
    hs = rmsnorm_bf16(xs, norm_mix_pre[0])
    outs = {k: [] for k in ("kp", "vp", "pp", "rp", "ip", "ks", "vs", "ps", "rs", "is")}
    t_seq = _tile(seq, 1024)
    n_seq_blk = seq // t_seq
    t_lift = _tile(seq, 512)
    n_lift = seq // t_lift
    ident = lambda j, i: (i, j)

    for l in range(depth):
        lam_init = 0.8 - 0.6 * math.exp(-0.3 * l)
        a_re, a_im, b_re, b_im, c_re, c_im = _s5_discretize(
            ssm_a_re[l], ssm_a_im[l], ssm_log_dt[l], ssm_b_re[l], ssm_b_im[l], ssm_c_re[l], ssm_c_im[l])
        s5prm = (a_re, a_im, b_re, b_im, c_re, c_im, ssm_d[l].reshape(1, ssm_w), glu_w[l].astype(BF16),
                 glu_b[l].reshape(1, ssm_w))
        lam_prm = (lambda_q1[l], lambda_k1[l], lambda_q2[l], lambda_k2[l])
        w_next = norm_mix_pre[l + 1] if l + 1 < depth else None
        pool_w_l = pool_w[l].astype(BF16)

        skip_u = lambda j: j + u_nblk * (j >= u_blk)
        zs, z = project(hs, hp, w_in, l, tm=t_seq, tn=tn, n_blocks=in_width // tn - u_nblk, w_col=skip_u,
                        s_shape=(rs, in_width), s_map=lambda j: (0, skip_u(j)),
                        p_shape=(mp, in_width), p_map=lambda j, i: (i, skip_u(j)))
        us, u_tb = project(hs, hp, w_in, l, tm=t_seq, tn=tn, n_blocks=u_nblk, w_col=lambda j: u_blk + j,
                           s_shape=(rs, ssm_w), s_map=lambda j: (0, j),
                           p_shape=(seq, bp * ssm_w),
                           p_map=lambda j, i: (i % n_seq_blk, (i // n_seq_blk) * u_nblk + j))

        q, k, kf, v, vf = rope_split(z, cos_p, sin_p, q_col0=q_col0, width=attn_qw, head_dim=hd, tm=512)
        o_pool, pbuf = pool_mixer_prompt(z.reshape(bp, seq, -1), pool_w_l, pool_scale[l])
        o_attn = diff_attention_prompt(q, k, v, lam_prm, subln_w[l], batch=bp, lam_init=lam_init, t=ATTN_TILE)
        zero_h = jnp.zeros((SUBLANES, n_state), F32)
        o_ssm, hre, him = s5_mixer_prompt(u_tb.reshape(seq * bp, ssm_w), zero_h, zero_h, s5prm, bp,
                                          t_chunk=S5_TIME_CHUNK)

        qs, _, kfs, _, vfs = rope_split(zs, cos_s, sin_s, q_col0=q_col0, width=attn_qw, head_dim=hd, tm=rs)
        buf_t = pad_s(state_pool[l]).transpose(1, 0, 2)
        o_pool_s, nbuf_t, o_ssm_s, sre, sim = decode_mixers(
            zs, us, buf_t, pool_w_l, pool_scale[l],
            pad_s(state_ssm_re[l].reshape(db, n_state)), pad_s(state_ssm_im[l].reshape(db, n_state)),
            s5prm, n_past)
        o_attn_s = diff_attention_decode(
            qs[:db].reshape(db, n_heads2, hd), kfs[:db].reshape(db, n_heads2, hd), vfs[:db].reshape(db, n_heads, vd),
            cache_k, cache_v, l, page_table, lam_prm, subln_w[l], lam_init=lam_init)

        branches_s = (o_pool_s.astype(BF16), pad_s(o_attn_s.reshape(db, attn_vw)).astype(BF16), o_ssm_s.astype(BF16))
        branches_p = (o_pool.reshape(mp, pool_width), o_attn, o_ssm.reshape(seq, bp * ssm_w))
        merged_s, merged = lift_merge(branches_s, zs, branches_p, z, lambda i: (i % n_lift, i // n_lift),
                                      w_lift, l, gate_col0, tm=t_lift, tn=tn)
        ys, y = project(merged_s, merged, w_out, l, tm=1024, tn=tn, n_blocks=d_model // tn, w_col=lambda j: j,
                        s_shape=(rs, d_model), s_map=lambda j: (0, j), p_shape=(mp, d_model), p_map=ident)
        xs, h2s = residual_norm(xs, ys, norm_mix_post[l], norm_ffn_pre[l])
        xp, h2 = residual_norm(xp, y, norm_mix_post[l], norm_ffn_pre[l])
        act_s, act = swiglu(h2s, h2, ffn_w_gate, ffn_w_up, l, tm=1024, tn=256)
        w_down = ffn_w_down[l].astype(BF16)
        fs = matmul_ksplit(act_s, w_down, tm=rs, tn=512, tk=d_ff // 2)
        f = matmul_ksplit(act, w_down, tm=1024, tn=512, tk=d_ff // 2)
        xs, hs = residual_norm(xs, fs, norm_ffn_post[l], w_next)
        xp, hp = residual_norm(xp, f, norm_ffn_post[l], w_next)

        outs["kp"].append(kf.reshape(bp, seq, n_heads2, hd))
        outs["vp"].append(vf.reshape(bp, seq, n_heads, vd))
        outs["pp"].append(pbuf)
        outs["rp"].append(hre[SUBLANES - bp:].reshape(bp, n_groups, n_st))
        outs["ip"].append(him[SUBLANES - bp:].reshape(bp, n_groups, n_st))
        outs["ks"].append(kfs[:db].reshape(db, 1, n_heads2, hd))
        outs["vs"].append(vfs[:db].reshape(db, 1, n_heads, vd))
        outs["ps"].append(nbuf_t.transpose(1, 0, 2)[:db])
        outs["rs"].append(sre[:db].reshape(db, n_groups, n_st))
        outs["is"].append(sim[:db].reshape(db, n_groups, n_st))

    st = lambda key: jnp.stack(outs[key])
    return (xp.reshape(bp, seq, d_model), xs[:db].reshape(db, 1, d_model),
            st("kp"), st("vp"), st("pp"), st("rp"), st("ip"),
            st("ks"), st("vs"), st("ps"), st("rs"), st("is"))
```

```python
import functools
import math

import jax
import jax.numpy as jnp
from jax import lax
from jax.experimental import pallas as pl
from jax.experimental.pallas import tpu as pltpu

F32 = jnp.float32
BF16 = jnp.bfloat16

V7X_VMEM_BYTES = 64 * 1024 * 1024
VMEM_LIMIT_BYTES = V7X_VMEM_BYTES - 8 * 1024 * 1024
SUBLANES = 8
LANES = 128
BF16_ROWS = 16

NORM_EPS = 1e-6
SUBLN_EPS = 1e-5
ROPE_THETA = 10000.0
POOL_WINDOWS = (2, 4, 8, 16)
POOL_BUF = max(POOL_WINDOWS) - 1
SSM_BLOCKS = 4
ATTN_TILE = 512
S5_TIME_CHUNK = 64
DECODE_PAGES_PER_STEP = 4
CAST_CHUNK = 1024


def _params(*sem):
    return pltpu.CompilerParams(dimension_semantics=sem, vmem_limit_bytes=VMEM_LIMIT_BYTES)


def _tile(n, pref):
    t = min(n, pref)
    while n % t:
        t //= 2
    return t


def _rms(x, w, eps):
    return x * lax.rsqrt(jnp.mean(x * x, axis=-1, keepdims=True) + eps) * w


def _rmsnorm_kernel(x_ref, w_ref, o_ref):
    o_ref[...] = _rms(x_ref[...], w_ref[...], NORM_EPS).astype(o_ref.dtype)


def rmsnorm_bf16(x, w):
    m, d = x.shape
    tm = _tile(m, 256)
    return pl.pallas_call(
        _rmsnorm_kernel,
        grid=(m // tm,),
        in_specs=[pl.BlockSpec((tm, d), lambda i: (i, 0)), pl.BlockSpec((1, d), lambda i: (0, 0))],
        out_specs=pl.BlockSpec((tm, d), lambda i: (i, 0)),
        out_shape=jax.ShapeDtypeStruct((m, d), BF16),
        compiler_params=_params("parallel"),
        name="rmsnorm",
    )(x, w.reshape(1, d))


def _resnorm_kernel(x_ref, y_ref, wpost_ref, wnext_ref, xo_ref, ho_ref):
    x = x_ref[...] + _rms(y_ref[...], wpost_ref[...], NORM_EPS)
    xo_ref[...] = x
    ho_ref[...] = _rms(x, wnext_ref[...], NORM_EPS).astype(ho_ref.dtype)


def _resnorm_last_kernel(x_ref, y_ref, wpost_ref, xo_ref):
    xo_ref[...] = x_ref[...] + _rms(y_ref[...], wpost_ref[...], NORM_EPS)


def residual_norm(x, y, w_post, w_next):
    m, d = x.shape
    tm = _tile(m, 256)
    row = pl.BlockSpec((tm, d), lambda i: (i, 0))
    vec = pl.BlockSpec((1, d), lambda i: (0, 0))
    if w_next is None:
        return pl.pallas_call(
            _resnorm_last_kernel, grid=(m // tm,), in_specs=[row, row, vec], out_specs=row,
            out_shape=jax.ShapeDtypeStruct((m, d), F32), compiler_params=_params("parallel"),
            name="resnorm_last",
        )(x, y, w_post.reshape(1, d)), None
    return pl.pallas_call(
        _resnorm_kernel, grid=(m // tm,), in_specs=[row, row, vec, vec], out_specs=[row, row],
        out_shape=[jax.ShapeDtypeStruct((m, d), F32), jax.ShapeDtypeStruct((m, d), BF16)],
        compiler_params=_params("parallel"), name="resnorm",
    )(x, y, w_post.reshape(1, d), w_next.reshape(1, d))


def _first_row_step():
    return pl.program_id(1) == 0


def _cast_chunks(w_ref, wbf_ref):
    k = w_ref.shape[0]
    kc = _tile(k, CAST_CHUNK)
    for c0 in range(0, k, kc):
        wb = w_ref[c0:c0 + kc, :].astype(BF16)
        wbf_ref[c0:c0 + kc, :] = wb
        yield slice(c0, c0 + kc), wb


def _proj_kernel(xs_ref, xp_ref, w_ref, os_ref, op_ref, wbf_ref):
    @pl.when(_first_row_step())
    def _():
        acc_s = acc_p = 0.0
        for rows, wb in _cast_chunks(w_ref, wbf_ref):
            acc_p = acc_p + jnp.dot(xp_ref[:, rows], wb, preferred_element_type=F32)
            acc_s = acc_s + jnp.dot(xs_ref[:, rows], wb, preferred_element_type=F32)
        os_ref[...] = acc_s.astype(os_ref.dtype)
        op_ref[...] = acc_p.astype(op_ref.dtype)

    @pl.when(jnp.logical_not(_first_row_step()))
    def _():
        op_ref[...] = jnp.dot(xp_ref[...], wbf_ref[...], preferred_element_type=F32).astype(op_ref.dtype)


def project(xs, xp, w, layer, *, tm, tn, n_blocks, w_col, s_shape, s_map, p_shape, p_map, out_dtype=F32):
    rs, k = xs.shape
    tm = _tile(xp.shape[0], tm)
    return pl.pallas_call(
        _proj_kernel,
        grid=(n_blocks, xp.shape[0] // tm),
        in_specs=[pl.BlockSpec((rs, k), lambda j, i: (0, 0)),
                  pl.BlockSpec((tm, k), lambda j, i: (i, 0)),
                  pl.BlockSpec((None, k, tn), lambda j, i: (layer, 0, w_col(j)))],
        out_specs=[pl.BlockSpec((rs, tn), lambda j, i: s_map(j)),
                   pl.BlockSpec((tm, tn), p_map)],
        out_shape=[jax.ShapeDtypeStruct(s_shape, out_dtype), jax.ShapeDtypeStruct(p_shape, out_dtype)],
        scratch_shapes=[pltpu.VMEM((k, tn), BF16)],
        compiler_params=_params("parallel", "arbitrary"),
        name="project",
    )(xs, xp, w)


def _silu_mul(g, u):
    return g * jax.nn.sigmoid(g) * u


def _swiglu_kernel(xs_ref, xp_ref, wg_ref, wu_ref, os_ref, op_ref, wgbf_ref, wubf_ref):
    @pl.when(_first_row_step())
    def _():
        gs = gp = us = up = 0.0
        for (rows, wg), (_, wu) in zip(_cast_chunks(wg_ref, wgbf_ref), _cast_chunks(wu_ref, wubf_ref)):
            xp, xs = xp_ref[:, rows], xs_ref[:, rows]
            gp = gp + jnp.dot(xp, wg, preferred_element_type=F32)
            up = up + jnp.dot(xp, wu, preferred_element_type=F32)
            gs = gs + jnp.dot(xs, wg, preferred_element_type=F32)
            us = us + jnp.dot(xs, wu, preferred_element_type=F32)
        os_ref[...] = _silu_mul(gs, us).astype(os_ref.dtype)
        op_ref[...] = _silu_mul(gp, up).astype(op_ref.dtype)

    @pl.when(jnp.logical_not(_first_row_step()))
    def _():
        x = xp_ref[...]
        g = jnp.dot(x, wgbf_ref[...], preferred_element_type=F32)
        u = jnp.dot(x, wubf_ref[...], preferred_element_type=F32)
        op_ref[...] = _silu_mul(g, u).astype(op_ref.dtype)


def swiglu(xs, xp, wg, wu, layer, *, tm, tn):
    rs, k = xs.shape
    m = xp.shape[0]
    n = wg.shape[2]
    tm = _tile(m, tm)
    wspec = pl.BlockSpec((None, k, tn), lambda j, i: (layer, 0, j))
    return pl.pallas_call(
        _swiglu_kernel,
        grid=(n // tn, m // tm),
        in_specs=[pl.BlockSpec((rs, k), lambda j, i: (0, 0)),
                  pl.BlockSpec((tm, k), lambda j, i: (i, 0)), wspec, wspec],
        out_specs=[pl.BlockSpec((rs, tn), lambda j, i: (0, j)),
                   pl.BlockSpec((tm, tn), lambda j, i: (i, j))],
        out_shape=[jax.ShapeDtypeStruct((rs, n), BF16), jax.ShapeDtypeStruct((m, n), BF16)],
        scratch_shapes=[pltpu.VMEM((k, tn), BF16), pltpu.VMEM((k, tn), BF16)],
        compiler_params=_params("parallel", "arbitrary"),
        name="swiglu",
    )(xs, xp, wg, wu)


def _merge(o_pool_ref, o_attn_ref, o_ssm_ref, g0_ref, g1_ref, g2_ref, wbf_ref):
    pw, aw = o_pool_ref.shape[1], o_attn_ref.shape[1]
    up = jnp.dot(o_pool_ref[...], wbf_ref[0:pw, :], preferred_element_type=F32)
    ua = jnp.dot(o_attn_ref[...], wbf_ref[pw:pw + aw, :], preferred_element_type=F32)
    us = jnp.dot(o_ssm_ref[...], wbf_ref[pw + aw:, :], preferred_element_type=F32)
    return (jax.nn.sigmoid(g0_ref[...]) * up + jax.nn.sigmoid(g1_ref[...]) * ua
            + jax.nn.sigmoid(g2_ref[...]) * us)


def _lift_kernel(*refs):
    s_in, p_in, (w_ref, os_ref, op_ref, wbf_ref) = refs[0:6], refs[6:12], refs[12:]

    @pl.when(_first_row_step())
    def _():
        wbf_ref[...] = w_ref[...].astype(BF16)
        os_ref[...] = _merge(*s_in, wbf_ref).astype(os_ref.dtype)

    op_ref[...] = _merge(*p_in, wbf_ref).astype(op_ref.dtype)


def lift_merge(branches_s, zs, branches_p, zp, ssm_map_p, w_lift, layer, gate_col0, *, tm, tn):
    rs = zs.shape[0]
    m = zp.shape[0]
    k, d = w_lift.shape[1], w_lift.shape[2]
    tm = _tile(m, tm)
    gb, nb = gate_col0 // tn, d // tn
    widths = [o.shape[1] for o in branches_s]
    s_specs = [pl.BlockSpec((rs, wd), lambda j, i: (0, 0)) for wd in widths]
    s_gates = [pl.BlockSpec((rs, tn), lambda j, i, br=br: (0, gb + br * nb + j)) for br in range(3)]
    p_specs = [pl.BlockSpec((tm, widths[0]), lambda j, i: (i, 0)),
               pl.BlockSpec((tm, widths[1]), lambda j, i: (i, 0)),
               pl.BlockSpec((tm, widths[2]), lambda j, i: ssm_map_p(i))]
    p_gates = [pl.BlockSpec((tm, tn), lambda j, i, br=br: (i, gb + br * nb + j)) for br in range(3)]
    return pl.pallas_call(
        _lift_kernel,
        grid=(nb, m // tm),
        in_specs=s_specs + s_gates + p_specs + p_gates
        + [pl.BlockSpec((None, k, tn), lambda j, i: (layer, 0, j))],
        out_specs=[pl.BlockSpec((rs, tn), lambda j, i: (0, j)), pl.BlockSpec((tm, tn), lambda j, i: (i, j))],
        out_shape=[jax.ShapeDtypeStruct((rs, d), BF16), jax.ShapeDtypeStruct((m, d), BF16)],
        scratch_shapes=[pltpu.VMEM((k, tn), BF16)],
        compiler_params=_params("parallel", "arbitrary"),
        name="lift_merge",
    )(*branches_s, zs, zs, zs, *branches_p, zp, zp, zp, w_lift)


def _mm_acc_kernel(x_ref, w_ref, o_ref, acc_ref):
    kk = pl.program_id(2)

    @pl.when(kk == 0)
    def _():
        acc_ref[...] = jnp.zeros_like(acc_ref)

    acc_ref[...] += jnp.dot(x_ref[...], w_ref[...], preferred_element_type=F32)

    @pl.when(kk == pl.num_programs(2) - 1)
    def _():
        o_ref[...] = acc_ref[...]


def matmul_ksplit(x, w, *, tm, tn, tk):
    m, k = x.shape
    n = w.shape[1]
    tm = _tile(m, tm)
    return pl.pallas_call(
        _mm_acc_kernel,
        grid=(m // tm, n // tn, k // tk),
        in_specs=[pl.BlockSpec((tm, tk), lambda i, j, kk: (i, kk)),
                  pl.BlockSpec((tk, tn), lambda i, j, kk: (kk, j))],
        out_specs=pl.BlockSpec((tm, tn), lambda i, j, kk: (i, j)),
        out_shape=jax.ShapeDtypeStruct((m, n), F32),
        scratch_shapes=[pltpu.VMEM((tm, tn), F32)],
        compiler_params=_params("parallel", "parallel", "arbitrary"),
        name="matmul_ksplit",
    )(x, w)


def _rope_tables(pos, head_dim):
    half = head_dim // 2
    inv = ROPE_THETA ** (-jnp.arange(half, dtype=F32) / half)
    ang = pos.astype(F32)[:, None] * inv[None, :]
    cos, sin = jnp.cos(ang), jnp.sin(ang)
    return jnp.concatenate([cos, cos], axis=-1), jnp.concatenate([-sin, sin], axis=-1)


def _rope_kernel(q_ref, k_ref, v_ref, cos_ref, sin_ref, qo_ref, ko_ref, kf_ref, vo_ref, vf_ref, *, head_dim):
    cos, sin = cos_ref[...], sin_ref[...]
    for h in range(q_ref.shape[1] // head_dim):
        cols = slice(h * head_dim, (h + 1) * head_dim)
        for src, outs in ((q_ref, (qo_ref,)), (k_ref, (ko_ref, kf_ref))):
            x = src[:, cols]
            r = x * cos + pltpu.roll(x, head_dim // 2, 1) * sin
            for o in outs:
                o[:, cols] = r.astype(o.dtype)
    v = v_ref[...]
    vo_ref[...] = v.astype(vo_ref.dtype)
    vf_ref[...] = v


def rope_split(z, cos, sin, *, q_col0, width, head_dim, tm):
    m = z.shape[0]
    s_tab = cos.shape[0]
    tm = _tile(min(m, s_tab), tm)
    tc = 1024
    nc = width // tc
    qb = q_col0 // tc
    n_tab = s_tab // tm

    def zspec(off):
        return pl.BlockSpec((tm, tc), lambda i, j: (i, off + j))

    tab = pl.BlockSpec((tm, head_dim), lambda i, j: (i % n_tab, 0))
    out = pl.BlockSpec((tm, tc), lambda i, j: (i, j))
    return pl.pallas_call(
        functools.partial(_rope_kernel, head_dim=head_dim),
        grid=(m // tm, nc),
        in_specs=[zspec(qb), zspec(qb + nc), zspec(qb + 2 * nc), tab, tab],
        out_specs=[out] * 5,
        out_shape=[jax.ShapeDtypeStruct((m, width), dt) for dt in (BF16, BF16, F32, BF16, F32)],
        compiler_params=_params("parallel", "arbitrary"),
        name="rope_split",
    )(z, z, z, cos, sin)


def _pool_features(ext_load, p, pos0, gw):
    s = p.shape[0]
    pos = pos0 + lax.broadcasted_iota(jnp.int32, (s, 1), 0)
    feats = []
    for g, w in enumerate(POOL_WINDOWS):
        cols = slice(g * gw, (g + 1) * gw)
        win = p[:, cols]
        for j in range(1, w):
            win = win + ext_load(j, cols)
        cnt = jnp.minimum(pos + 1, w).astype(F32)
        feats.append(win / cnt - p[:, cols])
    return feats


def _pool_kernel(p_ref, w_ref, scale_ref, o_ref, buf_ref, ext_ref, *, s, gw):
    pad = BF16_ROWS
    ext_ref[0:pad, :] = jnp.zeros((pad, ext_ref.shape[1]), F32)
    ext_ref[pad:pad + s, :] = p_ref[...]
    p = p_ref[...]
    feats = _pool_features(lambda j, cols: ext_ref[pad - j:pad - j + s, cols], p, 0, gw)
    for g, f in enumerate(feats):
        cols = slice(g * gw, (g + 1) * gw)
        o = jnp.dot(f.astype(BF16), w_ref[g], preferred_element_type=F32) * scale_ref[:, cols]
        o_ref[:, cols] = o.astype(o_ref.dtype)
    buf_ref[...] = ext_ref[pad + s - POOL_BUF:pad + s, :]


def pool_mixer_prompt(z3, pool_w, pool_scale):
    b, s, _ = z3.shape
    ng, gw, _ = pool_w.shape
    width = ng * gw
    return pl.pallas_call(
        functools.partial(_pool_kernel, s=s, gw=gw),
        grid=(b,),
        in_specs=[pl.BlockSpec((None, s, width), lambda i: (i, 0, 0)),
                  pl.BlockSpec((ng, gw, gw), lambda i: (0, 0, 0)),
                  pl.BlockSpec((1, width), lambda i: (0, 0))],
        out_specs=[pl.BlockSpec((None, s, width), lambda i: (i, 0, 0)),
                   pl.BlockSpec((None, POOL_BUF, width), lambda i: (i, 0, 0))],
        out_shape=[jax.ShapeDtypeStruct((b, s, width), BF16),
                   jax.ShapeDtypeStruct((b, POOL_BUF, width), F32)],
        scratch_shapes=[pltpu.VMEM((BF16_ROWS + s, width), F32)],
        compiler_params=_params("parallel"),
        name="pool_mixer",
    )(z3, pool_w, pool_scale.reshape(1, width))


def _s5_discretize(a_re, a_im, log_dt, b_re, b_im, c_re, c_im):
    g, p = a_re.shape
    c = b_re.shape[-1]
    a = lax.complex(a_re.astype(F32), a_im.astype(F32))
    dt = jnp.exp(log_dt.astype(F32))[:, None]
    a_bar = jnp.exp(a * dt)
    b_bar = ((a_bar - 1.0) / a)[..., None] * lax.complex(b_re.astype(F32), b_im.astype(F32))
    gb = g // SSM_BLOCKS
    eye = jnp.eye(gb, dtype=F32)

    def b_blocks(x):
        x = x.reshape(SSM_BLOCKS, gb, p, c)
        return jnp.einsum('kgpc,gh->kgchp', x, eye).reshape(SSM_BLOCKS, gb * c, gb * p).astype(BF16)

    def c_blocks(x):
        x = x.reshape(SSM_BLOCKS, gb, c, p)
        return jnp.einsum('kgcp,gh->kgphc', x, eye).reshape(SSM_BLOCKS, gb * p, gb * c).astype(BF16)

    return (jnp.real(a_bar).reshape(1, g * p), jnp.imag(a_bar).reshape(1, g * p),
            b_blocks(jnp.real(b_bar)), b_blocks(jnp.imag(b_bar)),
            c_blocks(c_re.astype(F32)), c_blocks(-c_im.astype(F32)))


def _s5_input_proj(u_bf16, br_ref, bi_ref, store_re, store_im):
    cw, sw = br_ref.shape[1], br_ref.shape[2]
    for k in range(SSM_BLOCKS):
        uk = u_bf16[:, k * cw:(k + 1) * cw]
        store_re(k * sw, sw, jnp.dot(uk, br_ref[k], preferred_element_type=F32))
        store_im(k * sw, sw, jnp.dot(uk, bi_ref[k], preferred_element_type=F32))


def _s5_output(hs_re, hs_im, u, cr_ref, ci_ref, d_ref, gw_ref, gb_ref):
    sw, cw = cr_ref.shape[1], cr_ref.shape[2]
    ys = []
    for k in range(SSM_BLOCKS):
        ys.append(jnp.dot(hs_re(k * sw, sw).astype(BF16), cr_ref[k], preferred_element_type=F32)
                  + jnp.dot(hs_im(k * sw, sw).astype(BF16), ci_ref[k], preferred_element_type=F32))
    y = jnp.concatenate(ys, axis=-1) + d_ref[...] * u
    zz = jax.nn.gelu(y, approximate=True)
    gate = jnp.dot(zz.astype(BF16), gw_ref[...], preferred_element_type=F32) + gb_ref[...]
    return zz * jax.nn.sigmoid(gate)


def _s5_kernel(u_ref, h0r_ref, h0i_ref, ar_ref, ai_ref, br_ref, bi_ref, cr_ref, ci_ref, d_ref, gw_ref, gb_ref,
               o_ref, hr_ref, hi_ref, xr_sc, xi_sc, *, nb, lane_chunk):
    rows, n_state = xr_sc.shape
    steps = SUBLANES // nb

    @pl.when(pl.program_id(0) == 0)
    def _():
        hr_ref[...] = h0r_ref[...]
        hi_ref[...] = h0i_ref[...]

    u = u_ref[...]

    def store(sc):
        def f(c0, w, val):
            sc[:, c0:c0 + w] = val
        return f

    _s5_input_proj(u.astype(BF16), br_ref, bi_ref, store(xr_sc), store(xi_sc))

    row = lax.broadcasted_iota(jnp.int32, (SUBLANES, lane_chunk), 0)
    for c0 in range(0, n_state, lane_chunk):
        cols = slice(c0, c0 + lane_chunk)
        ar = jnp.broadcast_to(ar_ref[:, cols], (SUBLANES, lane_chunk))
        ai = jnp.broadcast_to(ai_ref[:, cols], (SUBLANES, lane_chunk))

        def tile(i, carry, cols=cols, ar=ar, ai=ai):
            hr, hi = carry
            r0 = pl.multiple_of(i * SUBLANES, SUBLANES)
            xr = xr_sc[pl.ds(r0, SUBLANES), cols]
            xi = xi_sc[pl.ds(r0, SUBLANES), cols]
            out_r, out_i = xr, xi
            for k in range(steps):
                pr = pltpu.roll(hr, nb, 0) if nb < SUBLANES else hr
                pi = pltpu.roll(hi, nb, 0) if nb < SUBLANES else hi
                hr = ar * pr - ai * pi + xr
                hi = ar * pi + ai * pr + xi
                sel = (row >= k * nb) & (row < (k + 1) * nb)
                out_r = jnp.where(sel, hr, out_r)
                out_i = jnp.where(sel, hi, out_i)
            xr_sc[pl.ds(r0, SUBLANES), cols] = out_r
            xi_sc[pl.ds(r0, SUBLANES), cols] = out_i
            return hr, hi

        hr, hi = lax.fori_loop(0, rows // SUBLANES, tile, (hr_ref[:, cols], hi_ref[:, cols]))
        hr_ref[:, cols] = hr
        hi_ref[:, cols] = hi

    out = _s5_output(lambda c0, w: xr_sc[:, c0:c0 + w], lambda c0, w: xi_sc[:, c0:c0 + w],
                     u, cr_ref, ci_ref, d_ref, gw_ref, gb_ref)
    o_ref[...] = out.astype(o_ref.dtype)


def s5_mixer_prompt(u_tb, h0_re, h0_im, prm, nb, *, t_chunk):
    n_rows, cw = u_tb.shape
    a_re, a_im, b_re, b_im, c_re, c_im, d, glu_w, glu_b = prm
    n_state = a_re.shape[1]
    rows = t_chunk * nb
    const2 = lambda i: (0, 0)
    const3 = lambda i: (0, 0, 0)
    return pl.pallas_call(
        functools.partial(_s5_kernel, nb=nb, lane_chunk=512),
        grid=(n_rows // rows,),
        in_specs=[pl.BlockSpec((rows, cw), lambda i: (i, 0)),
                  pl.BlockSpec((SUBLANES, n_state), const2), pl.BlockSpec((SUBLANES, n_state), const2),
                  pl.BlockSpec((1, n_state), const2), pl.BlockSpec((1, n_state), const2),
                  pl.BlockSpec(b_re.shape, const3), pl.BlockSpec(b_im.shape, const3),
                  pl.BlockSpec(c_re.shape, const3), pl.BlockSpec(c_im.shape, const3),
                  pl.BlockSpec((1, cw), const2), pl.BlockSpec((cw, cw), const2), pl.BlockSpec((1, cw), const2)],
        out_specs=[pl.BlockSpec((rows, cw), lambda i: (i, 0)),
                   pl.BlockSpec((SUBLANES, n_state), const2), pl.BlockSpec((SUBLANES, n_state), const2)],
        out_shape=[jax.ShapeDtypeStruct((n_rows, cw), BF16),
                   jax.ShapeDtypeStruct((SUBLANES, n_state), F32),
                   jax.ShapeDtypeStruct((SUBLANES, n_state), F32)],
        scratch_shapes=[pltpu.VMEM((rows, n_state), F32), pltpu.VMEM((rows, n_state), F32)],
        compiler_params=_params("arbitrary"),
        name="s5_mixer",
    )(u_tb, h0_re, h0_im, a_re, a_im, b_re, b_im, c_re, c_im, d, glu_w, glu_b)


def _decode_mix_kernel(p_ref, u_ref, buf_ref, pw_ref, ps_ref, h0r_ref, h0i_ref, ar_ref, ai_ref,
                       br_ref, bi_ref, cr_ref, ci_ref, d_ref, gw_ref, gb_ref,
                       op_ref, nbuf_ref, os_ref, hr_ref, hi_ref, *, pos0, gw):
    p = p_ref[...]
    feats = []
    for g, w in enumerate(POOL_WINDOWS):
        cols = slice(g * gw, (g + 1) * gw)
        win = p[:, cols]
        for j in range(1, w):
            win = win + buf_ref[POOL_BUF - j, :, cols]
        feats.append(win / float(min(pos0 + 1, w)) - p[:, cols])
    for g, f in enumerate(feats):
        cols = slice(g * gw, (g + 1) * gw)
        o = jnp.dot(f.astype(BF16), pw_ref[g], preferred_element_type=F32) * ps_ref[:, cols]
        op_ref[:, cols] = o
    for r in range(POOL_BUF - 1):
        nbuf_ref[r] = buf_ref[r + 1]
    nbuf_ref[POOL_BUF - 1] = p

    u = u_ref[...]

    def store(ref):
        def f(c0, w, val):
            ref[:, c0:c0 + w] = val
        return f

    _s5_input_proj(u.astype(BF16), br_ref, bi_ref, store(hr_ref), store(hi_ref))
    h0r, h0i = h0r_ref[...], h0i_ref[...]
    ar, ai = ar_ref[...], ai_ref[...]
    hr = ar * h0r - ai * h0i + hr_ref[...]
    hi = ar * h0i + ai * h0r + hi_ref[...]
    hr_ref[...] = hr
    hi_ref[...] = hi
    os_ref[...] = _s5_output(lambda c0, w: hr_ref[:, c0:c0 + w], lambda c0, w: hi_ref[:, c0:c0 + w],
                             u, cr_ref, ci_ref, d_ref, gw_ref, gb_ref)


def decode_mixers(z, u, buf_t, pool_w, pool_scale, h0_re, h0_im, s5prm, pos0):
    r = z.shape[0]
    ng, gw, _ = pool_w.shape
    pw = ng * gw
    a_re, a_im, b_re, b_im, c_re, c_im, d, glu_w, glu_b = s5prm
    cw = d.shape[1]
    n_state = a_re.shape[1]
    full = lambda a: pl.BlockSpec(a.shape, lambda i: (0,) * a.ndim)
    args = (buf_t, pool_w, pool_scale.reshape(1, pw), h0_re, h0_im, a_re, a_im, b_re, b_im, c_re, c_im,
            d, glu_w, glu_b)
    return pl.pallas_call(
        functools.partial(_decode_mix_kernel, pos0=pos0, gw=gw),
        grid=(1,),
        in_specs=[pl.BlockSpec((r, pw), lambda i: (0, 0)),
                  pl.BlockSpec((r, cw), lambda i: (0, 0))] + [full(a) for a in args],
        out_specs=[pl.BlockSpec((r, pw), lambda i: (0, 0)),
                   pl.BlockSpec((POOL_BUF, r, pw), lambda i: (0, 0, 0)),
                   pl.BlockSpec((r, cw), lambda i: (0, 0)),
                   pl.BlockSpec((r, n_state), lambda i: (0, 0)),
                   pl.BlockSpec((r, n_state), lambda i: (0, 0))],
        out_shape=[jax.ShapeDtypeStruct((r, pw), F32),
                   jax.ShapeDtypeStruct((POOL_BUF, r, pw), F32),
                   jax.ShapeDtypeStruct((r, cw), F32),
                   jax.ShapeDtypeStruct((r, n_state), F32),
                   jax.ShapeDtypeStruct((r, n_state), F32)],
        compiler_params=_params("arbitrary"),
        name="decode_mixers",
    )(z, u, *args)


def _lambda(lq1_ref, lk1_ref, lq2_ref, lk2_ref, lam_init):
    return (jnp.exp(jnp.sum(lq1_ref[...] * lk1_ref[...], axis=-1, keepdims=True))
            - jnp.exp(jnp.sum(lq2_ref[...] * lk2_ref[...], axis=-1, keepdims=True)) + lam_init)


def _diff_finish(o1, o2, lam, sw, lam_init):
    o = o1 - lam * o2
    return _rms(o, sw, SUBLN_EPS) * (1.0 - lam_init)


def _attn_kernel(q_ref, k_ref, v_ref, lq1_ref, lk1_ref, lq2_ref, lk2_ref, sw_ref, o_ref,
                 m_sc, l_sc, acc_sc, *, t, hd, scale, lam_init):
    i = pl.program_id(2)
    q = q_ref[...]
    c_exp = scale * math.log2(math.e)
    m_sc[...] = jnp.full(m_sc.shape, -jnp.inf, F32)
    l_sc[...] = jnp.zeros(l_sc.shape, F32)
    acc_sc[...] = jnp.zeros(acc_sc.shape, F32)

    def kv_tile(j, diagonal):
        r0 = pl.multiple_of(j * t, t)
        kt = k_ref[pl.ds(r0, t), :]
        vt = v_ref[pl.ds(r0, t), :]
        for mp in range(2):
            s = lax.dot_general(kt[:, mp * hd:(mp + 1) * hd], q[:, mp * hd:(mp + 1) * hd],
                                (((1,), (1,)), ((), ())), preferred_element_type=F32)
            if diagonal:
                kk = lax.broadcasted_iota(jnp.int32, (t, t), 0)
                qq = lax.broadcasted_iota(jnp.int32, (t, t), 1)
                s = jnp.where(kk <= qq, s, -jnp.inf)
            m_old = m_sc[mp]
            m_new = jnp.maximum(m_old, jnp.max(s, axis=0, keepdims=True))
            alpha = jnp.exp2((m_old - m_new) * c_exp)
            p = jnp.exp2((s - m_new) * c_exp)
            l_sc[mp] = alpha * l_sc[mp] + jnp.sum(p, axis=0, keepdims=True)
            pv = lax.dot_general(vt, p.astype(BF16), (((0,), (0,)), ((), ())), preferred_element_type=F32)
            acc_sc[mp] = alpha * acc_sc[mp] + pv
            m_sc[mp] = m_new

    def body(j, c):
        kv_tile(j, False)
        return c

    lax.fori_loop(0, i, body, 0)
    kv_tile(i, True)
    lam = _lambda(lq1_ref, lk1_ref, lq2_ref, lk2_ref, lam_init)
    o = acc_sc[0] / l_sc[0] - lam * (acc_sc[1] / l_sc[1])
    o = o * lax.rsqrt(jnp.mean(o * o, axis=0, keepdims=True) + SUBLN_EPS) * sw_ref[...] * (1.0 - lam_init)
    o_ref[...] = o.T.astype(o_ref.dtype)


def diff_attention_prompt(q, k, v, lam_prm, subln_w, *, batch, lam_init, t):
    m, width = q.shape
    s = m // batch
    hd = lam_prm[0].shape[-1]
    vd = 2 * hd
    nh = width // vd
    t = _tile(s, t)
    nq = s // t
    vec = lambda n: pl.BlockSpec((1, n), lambda b, h, i: (0, 0))
    return pl.pallas_call(
        functools.partial(_attn_kernel, t=t, hd=hd, scale=hd ** -0.5, lam_init=lam_init),
        grid=(batch, nh, nq),
        in_specs=[pl.BlockSpec((t, vd), lambda b, h, i: (b * nq + i, h)),
                  pl.BlockSpec((s, vd), lambda b, h, i: (b, h)),
                  pl.BlockSpec((s, vd), lambda b, h, i: (b, h)),
                  vec(hd), vec(hd), vec(hd), vec(hd), pl.BlockSpec((vd, 1), lambda b, h, i: (0, 0))],
        out_specs=pl.BlockSpec((t, vd), lambda b, h, i: (b * nq + i, h)),
        out_shape=jax.ShapeDtypeStruct((m, width), BF16),
        scratch_shapes=[pltpu.VMEM((2, 1, t), F32), pltpu.VMEM((2, 1, t), F32), pltpu.VMEM((2, vd, t), F32)],
        compiler_params=_params("parallel", "parallel", "arbitrary"),
        name="diff_attention",
    )(q, k, v, *[p.reshape(1, hd) for p in lam_prm], subln_w.reshape(vd, 1))


def _dec_attn_kernel(pt_ref, q_ref, *refs, n_pp, c_exp, lam_init):
    kc_refs, vc_refs = refs[:n_pp], refs[n_pp:2 * n_pp]
    kn_ref, vn_ref, lq1_ref, lk1_ref, lq2_ref, lk2_ref, sw_ref, o_ref, m_sc, l_sc, acc_sc = refs[2 * n_pp:]
    pg = pl.program_id(1)
    nh, hd = m_sc.shape[1], m_sc.shape[2]
    ones = jnp.ones((hd, hd), BF16)
    twice = lambda x: jnp.concatenate([x, x], axis=-1)

    @pl.when(pg == 0)
    def _():
        m_sc[...] = jnp.full(m_sc.shape, -jnp.inf, F32)
        l_sc[...] = jnp.zeros(l_sc.shape, F32)
        acc_sc[...] = jnp.zeros(acc_sc.shape, F32)

    def update(keys, values):
        for mp in range(2):
            qm = q_ref[pl.ds(mp, nh, stride=2), :]
            s = []
            for k in keys:
                prod = k(mp) * qm[None]
                rows = prod.shape[0]
                rep = jnp.dot(prod.reshape(rows * nh, hd).astype(BF16), ones, preferred_element_type=F32)
                s.append(rep.reshape(rows, nh, hd))
            m_old = m_sc[mp]
            m_new = functools.reduce(jnp.maximum, [jnp.max(x, axis=0) for x in s], m_old)
            alpha = jnp.exp2((m_old - m_new) * c_exp)
            p = [jnp.exp2((x - m_new) * c_exp) for x in s]
            l_sc[mp] = alpha * l_sc[mp] + sum(jnp.sum(x, axis=0) for x in p)
            acc_sc[mp] = twice(alpha) * acc_sc[mp] + sum(jnp.sum(twice(x) * v, axis=0) for x, v in zip(p, values))
            m_sc[mp] = m_new

    update([lambda mp, r=r: r[:, pl.ds(mp, nh, stride=2), :] for r in kc_refs], [r[...] for r in vc_refs])

    @pl.when(pg == pl.num_programs(1) - 1)
    def _():
        update([lambda mp: kn_ref[pl.ds(mp, nh, stride=2), :][None]], [vn_ref[...][None]])
        lam = _lambda(lq1_ref, lk1_ref, lq2_ref, lk2_ref, lam_init)
        o_ref[...] = _diff_finish(acc_sc[0] / twice(l_sc[0]), acc_sc[1] / twice(l_sc[1]), lam, sw_ref[...],
                                  lam_init)


def diff_attention_decode(q, k_new, v_new, cache_k, cache_v, layer, page_table, lam_prm, subln_w, *, lam_init, n_pp):
    r, nj, hd = q.shape
    nh, vd = v_new.shape[1], v_new.shape[2]
    n_pages = page_table.shape[1]
    page = cache_k.shape[2]
    assert n_pages % n_pp == 0
    vec = lambda n: pl.BlockSpec((1, n), lambda b, p, pt: (0, 0))
    row_map = lambda b, p, pt: (b, 0, 0)

    def page_spec(heads, dim, t):
        return pl.BlockSpec((None, None, page, heads, dim),
                            lambda b, p, pt: (layer, pt[b * n_pages + p * n_pp + t], 0, 0, 0))

    grid_spec = pltpu.PrefetchScalarGridSpec(
        num_scalar_prefetch=1,
        grid=(r, n_pages // n_pp),
        in_specs=[pl.BlockSpec((None, nj, hd), row_map)]
        + [page_spec(nj, hd, t) for t in range(n_pp)] + [page_spec(nh, vd, t) for t in range(n_pp)]
        + [pl.BlockSpec((None, nj, hd), row_map), pl.BlockSpec((None, nh, vd), row_map),
           vec(hd), vec(hd), vec(hd), vec(hd), vec(vd)],
        out_specs=pl.BlockSpec((None, nh, vd), row_map),
        scratch_shapes=[pltpu.VMEM((2, nh, hd), F32), pltpu.VMEM((2, nh, hd), F32), pltpu.VMEM((2, nh, vd), F32)],
    )
    return pl.pallas_call(
        functools.partial(_dec_attn_kernel, n_pp=n_pp, c_exp=hd ** -0.5 * math.log2(math.e), lam_init=lam_init),
        grid_spec=grid_spec,
        out_shape=jax.ShapeDtypeStruct((r, nh, vd), F32),
        compiler_params=_params("parallel", "arbitrary"),
        name="diff_attention_decode",
    )(page_table.reshape(-1), q, *([cache_k] * n_pp), *([cache_v] * n_pp), k_new, v_new,
      *[p.reshape(1, hd) for p in lam_prm], subln_w.reshape(1, vd))


def kernel(x_prompt, x_sample, cache_k, cache_v, state_pool, state_ssm_re, state_ssm_im, page_table, norm_mix_pre, norm_mix_post, norm_ffn_pre, norm_ffn_post, w_in, pool_w, pool_scale, lambda_q1, lambda_k1, lambda_q2, lambda_k2, subln_w, ssm_a_re, ssm_a_im, ssm_b_re, ssm_b_im, ssm_c_re, ssm_c_im, ssm_d, ssm_log_dt, glu_w, glu_b, w_lift, w_out, ffn_w_gate, ffn_w_up, ffn_w_down):
    bp, seq, d_model = x_prompt.shape
    db, dseq, _ = x_sample.shape
    assert dseq == 1
    depth = w_in.shape[0]
    page = cache_k.shape[2]
    n_past = page_table.shape[1] * page
    n_heads2, hd = cache_k.shape[3], cache_k.shape[4]
    n_heads, vd = cache_v.shape[3], cache_v.shape[4]
    attn_qw = n_heads2 * hd
    attn_vw = n_heads * vd
    pool_width = pool_w.shape[1] * pool_w.shape[2]
    ssm_w = ssm_d.shape[1]
    n_groups, n_st = ssm_a_re.shape[1], ssm_a_re.shape[2]
    n_state = n_groups * n_st
    d_ff = ffn_w_gate.shape[-1]
    in_width = w_in.shape[2]
    q_col0 = pool_width
    u_col0 = pool_width + 2 * attn_qw + attn_vw
    gate_col0 = u_col0 + ssm_w
    tn = 512
    u_blk, u_nblk = u_col0 // tn, ssm_w // tn
    assert SUBLANES % bp == 0 and attn_qw == attn_vw and u_col0 % tn == 0 and ssm_w % tn == 0

    mp = bp * seq
    rs = BF16_ROWS
    xp = x_prompt.reshape(mp, d_model)
    xs = jnp.pad(x_sample.reshape(db, d_model), ((0, rs - db), (0, 0)))
    pad_s = lambda a: jnp.pad(a, ((0, rs - db),) + ((0, 0),) * (a.ndim - 1))

    cos_p, sin_p = _rope_tables(jnp.arange(seq, dtype=jnp.int32), hd)
    cos_s, sin_s = _rope_tables(jnp.full((rs,), n_past, jnp.int32), hd)

    hp = rmsnorm_bf16(xp, norm_mix_pre[0])
    hs = rmsnorm_bf16(xs, norm_mix_pre[0])
    outs = {k: [] for k in ("kp", "vp", "pp", "rp", "ip", "ks", "vs", "ps", "rs", "is")}
    t_seq = _tile(seq, 1024)
    n_seq_blk = seq // t_seq
    t_lift = _tile(seq, 512)
    n_lift = seq // t_lift
    ident = lambda j, i: (i, j)

    for l in range(depth):
        lam_init = 0.8 - 0.6 * math.exp(-0.3 * l)
        a_re, a_im, b_re, b_im, c_re, c_im = _s5_discretize(
            ssm_a_re[l], ssm_a_im[l], ssm_log_dt[l], ssm_b_re[l], ssm_b_im[l], ssm_c_re[l], ssm_c_im[l])
        s5prm = (a_re, a_im, b_re, b_im, c_re, c_im, ssm_d[l].reshape(1, ssm_w), glu_w[l].astype(BF16),
                 glu_b[l].reshape(1, ssm_w))
        lam_prm = (lambda_q1[l], lambda_k1[l], lambda_q2[l], lambda_k2[l])
        w_next = norm_mix_pre[l + 1] if l + 1 < depth else None
        pool_w_l = pool_w[l].astype(BF16)

        skip_u = lambda j: j + u_nblk * (j >= u_blk)
        zs, z = project(hs, hp, w_in, l, tm=t_seq, tn=tn, n_blocks=in_width // tn - u_nblk, w_col=skip_u,
                        s_shape=(rs, in_width), s_map=lambda j: (0, skip_u(j)),
                        p_shape=(mp, in_width), p_map=lambda j, i: (i, skip_u(j)))
        us, u_tb = project(hs, hp, w_in, l, tm=t_seq, tn=tn, n_blocks=u_nblk, w_col=lambda j: u_blk + j,
                           s_shape=(rs, ssm_w), s_map=lambda j: (0, j),
                           p_shape=(seq, bp * ssm_w),
                           p_map=lambda j, i: (i % n_seq_blk, (i // n_seq_blk) * u_nblk + j))

        q, k, kf, v, vf = rope_split(z, cos_p, sin_p, q_col0=q_col0, width=attn_qw, head_dim=hd, tm=512)
        o_pool, pbuf = pool_mixer_prompt(z.reshape(bp, seq, -1), pool_w_l, pool_scale[l])
        o_attn = diff_attention_prompt(q, k, v, lam_prm, subln_w[l], batch=bp, lam_init=lam_init, t=ATTN_TILE)
        zero_h = jnp.zeros((SUBLANES, n_state), F32)
        o_ssm, hre, him = s5_mixer_prompt(u_tb.reshape(seq * bp, ssm_w), zero_h, zero_h, s5prm, bp,
                                          t_chunk=S5_TIME_CHUNK)

        qs, _, kfs, _, vfs = rope_split(zs, cos_s, sin_s, q_col0=q_col0, width=attn_qw, head_dim=hd, tm=rs)
        buf_t = pad_s(state_pool[l]).transpose(1, 0, 2)
        o_pool_s, nbuf_t, o_ssm_s, sre, sim = decode_mixers(
            zs, us, buf_t, pool_w_l, pool_scale[l],
            pad_s(state_ssm_re[l].reshape(db, n_state)), pad_s(state_ssm_im[l].reshape(db, n_state)),
            s5prm, n_past)
        o_attn_s = diff_attention_decode(
            qs[:db].astype(F32).reshape(db, n_heads2, hd), kfs[:db].reshape(db, n_heads2, hd),
            vfs[:db].reshape(db, n_heads, vd), cache_k, cache_v, l, page_table, lam_prm, subln_w[l],
            lam_init=lam_init, n_pp=math.gcd(DECODE_PAGES_PER_STEP, page_table.shape[1]))

        branches_s = (o_pool_s.astype(BF16), pad_s(o_attn_s.reshape(db, attn_vw)).astype(BF16), o_ssm_s.astype(BF16))
        branches_p = (o_pool.reshape(mp, pool_width), o_attn, o_ssm.reshape(seq, bp * ssm_w))
        merged_s, merged = lift_merge(branches_s, zs, branches_p, z, lambda i: (i % n_lift, i // n_lift),
                                      w_lift, l, gate_col0, tm=t_lift, tn=tn)
        ys, y = project(merged_s, merged, w_out, l, tm=1024, tn=tn, n_blocks=d_model // tn, w_col=lambda j: j,
                        s_shape=(rs, d_model), s_map=lambda j: (0, j), p_shape=(mp, d_model), p_map=ident)
        xs, h2s = residual_norm(xs, ys, norm_mix_post[l], norm_ffn_pre[l])
        xp, h2 = residual_norm(xp, y, norm_mix_post[l], norm_ffn_pre[l])
        act_s, act = swiglu(h2s, h2, ffn_w_gate, ffn_w_up, l, tm=1024, tn=256)
        w_down = ffn_w_down[l].astype(BF16)
        fs = matmul_ksplit(act_s, w_down, tm=rs, tn=512, tk=d_ff // 2)
        f = matmul_ksplit(act, w_down, tm=1024, tn=512, tk=d_ff // 2)
        xs, hs = residual_norm(xs, fs, norm_ffn_post[l], w_next)
        xp, hp = residual_norm(xp, f, norm_ffn_post[l], w_next)

        outs["kp"].append(kf.reshape(bp, seq, n_heads2, hd))
        outs["vp"].append(vf.reshape(bp, seq, n_heads, vd))
        outs["pp"].append(pbuf)
        outs["rp"].append(hre[SUBLANES - bp:].reshape(bp, n_groups, n_st))
        outs["ip"].append(him[SUBLANES - bp:].reshape(bp, n_groups, n_st))
        outs["ks"].append(kfs[:db].reshape(db, 1, n_heads2, hd))
        outs["vs"].append(vfs[:db].reshape(db, 1, n_heads, vd))
        outs["ps"].append(nbuf_t.transpose(1, 0, 2)[:db])
        outs["rs"].append(sre[:db].reshape(db, n_groups, n_st))
        outs["is"].append(sim[:db].reshape(db, n_groups, n_st))

    st = lambda key: jnp.stack(outs[key])
    return (xp.reshape(bp, seq, d_model), xs[:db].reshape(db, 1, d_model),
            st("kp"), st("vp"), st("pp"), st("rp"), st("ip"),
            st("ks"), st("vs"), st("ps"), st("rs"), st("is"))
```

```python
import functools
import math

import jax
import jax.numpy as jnp
from jax import lax
from jax.experimental import pallas as pl
from jax.experimental.pallas import tpu as pltpu

F32 = jnp.float32
BF16 = jnp.bfloat16

V7X_VMEM_BYTES = 64 * 1024 * 1024
VMEM_LIMIT_BYTES = V7X_VMEM_BYTES - 8 * 1024 * 1024
SUBLANES = 8
LANES = 128
BF16_ROWS = 16

NORM_EPS = 1e-6
SUBLN_EPS = 1e-5
ROPE_THETA = 10000.0
POOL_WINDOWS = (2, 4, 8, 16)
POOL_BUF = max(POOL_WINDOWS) - 1
SSM_BLOCKS = 4
ATTN_TILE = 1024
S5_TIME_CHUNK = 64
DECODE_PAGES_PER_STEP = 4
CAST_CHUNK = 1024


def _params(*sem):
    return pltpu.CompilerParams(dimension_semantics=sem, vmem_limit_bytes=VMEM_LIMIT_BYTES)


def _tile(n, pref):
    t = min(n, pref)
    while n % t:
        t //= 2
    return t


def _rms(x, w, eps):
    return x * lax.rsqrt(jnp.mean(x * x, axis=-1, keepdims=True) + eps) * w


def _rmsnorm_kernel(x_ref, w_ref, o_ref):
    o_ref[...] = _rms(x_ref[...], w_ref[...], NORM_EPS).astype(o_ref.dtype)


def rmsnorm_bf16(x, w):
    m, d = x.shape
    tm = _tile(m, 256)
    return pl.pallas_call(
        _rmsnorm_kernel,
        grid=(m // tm,),
        in_specs=[pl.BlockSpec((tm, d), lambda i: (i, 0)), pl.BlockSpec((1, d), lambda i: (0, 0))],
        out_specs=pl.BlockSpec((tm, d), lambda i: (i, 0)),
        out_shape=jax.ShapeDtypeStruct((m, d), BF16),
        compiler_params=_params("parallel"),
        name="rmsnorm",
    )(x, w.reshape(1, d))


def _resnorm_kernel(x_ref, y_ref, wpost_ref, wnext_ref, xo_ref, ho_ref):
    x = x_ref[...] + _rms(y_ref[...], wpost_ref[...], NORM_EPS)
    xo_ref[...] = x
    ho_ref[...] = _rms(x, wnext_ref[...], NORM_EPS).astype(ho_ref.dtype)


def _resnorm_last_kernel(x_ref, y_ref, wpost_ref, xo_ref):
    xo_ref[...] = x_ref[...] + _rms(y_ref[...], wpost_ref[...], NORM_EPS)


def residual_norm(x, y, w_post, w_next):
    m, d = x.shape
    tm = _tile(m, 256)
    row = pl.BlockSpec((tm, d), lambda i: (i, 0))
    vec = pl.BlockSpec((1, d), lambda i: (0, 0))
    if w_next is None:
        return pl.pallas_call(
            _resnorm_last_kernel, grid=(m // tm,), in_specs=[row, row, vec], out_specs=row,
            out_shape=jax.ShapeDtypeStruct((m, d), F32), compiler_params=_params("parallel"),
            name="resnorm_last",
        )(x, y, w_post.reshape(1, d)), None
    return pl.pallas_call(
        _resnorm_kernel, grid=(m // tm,), in_specs=[row, row, vec, vec], out_specs=[row, row],
        out_shape=[jax.ShapeDtypeStruct((m, d), F32), jax.ShapeDtypeStruct((m, d), BF16)],
        compiler_params=_params("parallel"), name="resnorm",
    )(x, y, w_post.reshape(1, d), w_next.reshape(1, d))


def _first_row_step():
    return pl.program_id(1) == 0


def _cast_chunks(w_ref, wbf_ref):
    k = w_ref.shape[0]
    kc = _tile(k, CAST_CHUNK)
    for c0 in range(0, k, kc):
        wb = w_ref[c0:c0 + kc, :].astype(BF16)
        wbf_ref[c0:c0 + kc, :] = wb
        yield slice(c0, c0 + kc), wb


def _proj_kernel(xs_ref, xp_ref, w_ref, os_ref, op_ref, wbf_ref):
    @pl.when(_first_row_step())
    def _():
        acc_s = acc_p = 0.0
        for rows, wb in _cast_chunks(w_ref, wbf_ref):
            acc_p = acc_p + jnp.dot(xp_ref[:, rows], wb, preferred_element_type=F32)
            acc_s = acc_s + jnp.dot(xs_ref[:, rows], wb, preferred_element_type=F32)
        os_ref[...] = acc_s.astype(os_ref.dtype)
        op_ref[...] = acc_p.astype(op_ref.dtype)

    @pl.when(jnp.logical_not(_first_row_step()))
    def _():
        op_ref[...] = jnp.dot(xp_ref[...], wbf_ref[...], preferred_element_type=F32).astype(op_ref.dtype)


def project(xs, xp, w, layer, *, tm, tn, n_blocks, w_col, s_shape, s_map, p_shape, p_map, out_dtype=F32):
    rs, k = xs.shape
    tm = _tile(xp.shape[0], tm)
    return pl.pallas_call(
        _proj_kernel,
        grid=(n_blocks, xp.shape[0] // tm),
        in_specs=[pl.BlockSpec((rs, k), lambda j, i: (0, 0)),
                  pl.BlockSpec((tm, k), lambda j, i: (i, 0)),
                  pl.BlockSpec((None, k, tn), lambda j, i: (layer, 0, w_col(j)))],
        out_specs=[pl.BlockSpec((rs, tn), lambda j, i: s_map(j)),
                   pl.BlockSpec((tm, tn), p_map)],
        out_shape=[jax.ShapeDtypeStruct(s_shape, out_dtype), jax.ShapeDtypeStruct(p_shape, out_dtype)],
        scratch_shapes=[pltpu.VMEM((k, tn), BF16)],
        compiler_params=_params("parallel", "arbitrary"),
        name="project",
    )(xs, xp, w)


def _silu_mul(g, u):
    return g * jax.nn.sigmoid(g) * u


def _swiglu_kernel(xs_ref, xp_ref, wg_ref, wu_ref, os_ref, op_ref, wgbf_ref, wubf_ref):
    @pl.when(_first_row_step())
    def _():
        gs = gp = us = up = 0.0
        for (rows, wg), (_, wu) in zip(_cast_chunks(wg_ref, wgbf_ref), _cast_chunks(wu_ref, wubf_ref)):
            xp, xs = xp_ref[:, rows], xs_ref[:, rows]
            gp = gp + jnp.dot(xp, wg, preferred_element_type=F32)
            up = up + jnp.dot(xp, wu, preferred_element_type=F32)
            gs = gs + jnp.dot(xs, wg, preferred_element_type=F32)
            us = us + jnp.dot(xs, wu, preferred_element_type=F32)
        os_ref[...] = _silu_mul(gs, us).astype(os_ref.dtype)
        op_ref[...] = _silu_mul(gp, up).astype(op_ref.dtype)

    @pl.when(jnp.logical_not(_first_row_step()))
    def _():
        x = xp_ref[...]
        g = jnp.dot(x, wgbf_ref[...], preferred_element_type=F32)
        u = jnp.dot(x, wubf_ref[...], preferred_element_type=F32)
        op_ref[...] = _silu_mul(g, u).astype(op_ref.dtype)


def swiglu(xs, xp, wg, wu, layer, *, tm, tn):
    rs, k = xs.shape
    m = xp.shape[0]
    n = wg.shape[2]
    tm = _tile(m, tm)
    wspec = pl.BlockSpec((None, k, tn), lambda j, i: (layer, 0, j))
    return pl.pallas_call(
        _swiglu_kernel,
        grid=(n // tn, m // tm),
        in_specs=[pl.BlockSpec((rs, k), lambda j, i: (0, 0)),
                  pl.BlockSpec((tm, k), lambda j, i: (i, 0)), wspec, wspec],
        out_specs=[pl.BlockSpec((rs, tn), lambda j, i: (0, j)),
                   pl.BlockSpec((tm, tn), lambda j, i: (i, j))],
        out_shape=[jax.ShapeDtypeStruct((rs, n), BF16), jax.ShapeDtypeStruct((m, n), BF16)],
        scratch_shapes=[pltpu.VMEM((k, tn), BF16), pltpu.VMEM((k, tn), BF16)],
        compiler_params=_params("parallel", "arbitrary"),
        name="swiglu",
    )(xs, xp, wg, wu)


def _merge(o_pool_ref, o_attn_ref, o_ssm_ref, g0_ref, g1_ref, g2_ref, wbf_ref):
    pw, aw = o_pool_ref.shape[1], o_attn_ref.shape[1]
    up = jnp.dot(o_pool_ref[...], wbf_ref[0:pw, :], preferred_element_type=F32)
    ua = jnp.dot(o_attn_ref[...], wbf_ref[pw:pw + aw, :], preferred_element_type=F32)
    us = jnp.dot(o_ssm_ref[...], wbf_ref[pw + aw:, :], preferred_element_type=F32)
    return (jax.nn.sigmoid(g0_ref[...]) * up + jax.nn.sigmoid(g1_ref[...]) * ua
            + jax.nn.sigmoid(g2_ref[...]) * us)


def _lift_kernel(*refs):
    s_in, p_in, (w_ref, os_ref, op_ref, wbf_ref) = refs[0:6], refs[6:12], refs[12:]

    @pl.when(_first_row_step())
    def _():
        wbf_ref[...] = w_ref[...].astype(BF16)
        os_ref[...] = _merge(*s_in, wbf_ref).astype(os_ref.dtype)

    op_ref[...] = _merge(*p_in, wbf_ref).astype(op_ref.dtype)


def lift_merge(branches_s, zs, branches_p, zp, ssm_map_p, w_lift, layer, gate_col0, *, tm, tn):
    rs = zs.shape[0]
    m = zp.shape[0]
    k, d = w_lift.shape[1], w_lift.shape[2]
    tm = _tile(m, tm)
    gb, nb = gate_col0 // tn, d // tn
    widths = [o.shape[1] for o in branches_s]
    s_specs = [pl.BlockSpec((rs, wd), lambda j, i: (0, 0)) for wd in widths]
    s_gates = [pl.BlockSpec((rs, tn), lambda j, i, br=br: (0, gb + br * nb + j)) for br in range(3)]
    p_specs = [pl.BlockSpec((tm, widths[0]), lambda j, i: (i, 0)),
               pl.BlockSpec((tm, widths[1]), lambda j, i: (i, 0)),
               pl.BlockSpec((tm, widths[2]), lambda j, i: ssm_map_p(i))]
    p_gates = [pl.BlockSpec((tm, tn), lambda j, i, br=br: (i, gb + br * nb + j)) for br in range(3)]
    return pl.pallas_call(
        _lift_kernel,
        grid=(nb, m // tm),
        in_specs=s_specs + s_gates + p_specs + p_gates
        + [pl.BlockSpec((None, k, tn), lambda j, i: (layer, 0, j))],
        out_specs=[pl.BlockSpec((rs, tn), lambda j, i: (0, j)), pl.BlockSpec((tm, tn), lambda j, i: (i, j))],
        out_shape=[jax.ShapeDtypeStruct((rs, d), BF16), jax.ShapeDtypeStruct((m, d), BF16)],
        scratch_shapes=[pltpu.VMEM((k, tn), BF16)],
        compiler_params=_params("parallel", "arbitrary"),
        name="lift_merge",
    )(*branches_s, zs, zs, zs, *branches_p, zp, zp, zp, w_lift)


def _mm_acc_kernel(x_ref, w_ref, o_ref, acc_ref):
    kk = pl.program_id(2)

    @pl.when(kk == 0)
    def _():
        acc_ref[...] = jnp.zeros_like(acc_ref)

    acc_ref[...] += jnp.dot(x_ref[...], w_ref[...], preferred_element_type=F32)

    @pl.when(kk == pl.num_programs(2) - 1)
    def _():
        o_ref[...] = acc_ref[...]


def matmul_ksplit(x, w, *, tm, tn, tk):
    m, k = x.shape
    n = w.shape[1]
    tm = _tile(m, tm)
    return pl.pallas_call(
        _mm_acc_kernel,
        grid=(m // tm, n // tn, k // tk),
        in_specs=[pl.BlockSpec((tm, tk), lambda i, j, kk: (i, kk)),
                  pl.BlockSpec((tk, tn), lambda i, j, kk: (kk, j))],
        out_specs=pl.BlockSpec((tm, tn), lambda i, j, kk: (i, j)),
        out_shape=jax.ShapeDtypeStruct((m, n), F32),
        scratch_shapes=[pltpu.VMEM((tm, tn), F32)],
        compiler_params=_params("parallel", "parallel", "arbitrary"),
        name="matmul_ksplit",
    )(x, w)


def _rope_tables(pos, head_dim):
    half = head_dim // 2
    inv = ROPE_THETA ** (-jnp.arange(half, dtype=F32) / half)
    ang = pos.astype(F32)[:, None] * inv[None, :]
    cos, sin = jnp.cos(ang), jnp.sin(ang)
    return jnp.concatenate([cos, cos], axis=-1), jnp.concatenate([-sin, sin], axis=-1)


def _rope_kernel(q_ref, k_ref, v_ref, cos_ref, sin_ref, *rest, head_dim, stacked):
    qo_ref, ko_ref, kf_ref, vo_ref, vf_ref = rest[-5:]
    cos, sin = cos_ref[...], sin_ref[...]
    n_heads = q_ref.shape[1] // head_dim

    def rot(x):
        return x * cos + pltpu.roll(x, head_dim // 2, 1) * sin

    for h in range(n_heads):
        cols = slice(h * head_dim, (h + 1) * head_dim)
        qo_ref[:, cols] = rot(q_ref[:, cols]).astype(qo_ref.dtype)
        r = rot(k_ref[:, cols])
        ko_ref[:, cols] = r.astype(ko_ref.dtype)
        if stacked:
            kf_ref[:, h, :] = r
        else:
            kf_ref[:, cols] = r
    v = v_ref[...]
    vo_ref[...] = v.astype(vo_ref.dtype)
    if stacked:
        vd = vf_ref.shape[2]
        v_heads = v.shape[1] // vd
        h0 = pl.program_id(1) * v_heads
        for h in range(v_heads):
            vf_ref[:, pl.ds(h0 + h, 1), :] = v[:, None, h * vd:(h + 1) * vd]
    else:
        vf_ref[...] = v


def rope_split(z, cos, sin, *, q_col0, width, head_dim, tm, stack=None):
    m = z.shape[0]
    s_tab = cos.shape[0]
    tm = _tile(min(m, s_tab), tm)
    tc = 1024
    nc = width // tc
    qb = q_col0 // tc
    n_tab = s_tab // tm

    def zspec(off):
        return pl.BlockSpec((tm, tc), lambda i, j: (i, off + j))

    tab = pl.BlockSpec((tm, head_dim), lambda i, j: (i % n_tab, 0))
    out = pl.BlockSpec((tm, tc), lambda i, j: (i, j))
    in_specs = [zspec(qb), zspec(qb + nc), zspec(qb + 2 * nc), tab, tab]
    args = [z, z, z, cos, sin]
    aliases = {}
    if stack is None:
        f32_specs = [out, out]
        f32_shapes = [jax.ShapeDtypeStruct((m, width), F32)] * 2
    else:
        layer, depth, vd, k_all, v_all = stack
        kh, vh = tc // head_dim, width // vd
        f32_specs = [pl.BlockSpec((None, tm, kh, head_dim), lambda i, j: (layer, i, j, 0)),
                     pl.BlockSpec((None, tm, vh, vd), lambda i, j: (layer, i, 0, 0))]
        f32_shapes = [jax.ShapeDtypeStruct((depth, m, width // head_dim, head_dim), F32),
                      jax.ShapeDtypeStruct((depth, m, vh, vd), F32)]
        if k_all is not None:
            in_specs += [pl.BlockSpec(memory_space=pl.ANY)] * 2
            args += [k_all, v_all]
            aliases = {5: 2, 6: 4}
    return pl.pallas_call(
        functools.partial(_rope_kernel, head_dim=head_dim, stacked=stack is not None),
        grid=(m // tm, nc),
        in_specs=in_specs,
        out_specs=[out, out, f32_specs[0], out, f32_specs[1]],
        out_shape=[jax.ShapeDtypeStruct((m, width), BF16), jax.ShapeDtypeStruct((m, width), BF16), f32_shapes[0],
                   jax.ShapeDtypeStruct((m, width), BF16), f32_shapes[1]],
        input_output_aliases=aliases,
        compiler_params=_params("parallel", "arbitrary"),
        name="rope_split",
    )(*args)


def _pool_features(ext_load, p, pos0, gw):
    s = p.shape[0]
    pos = pos0 + lax.broadcasted_iota(jnp.int32, (s, 1), 0)
    feats = []
    for g, w in enumerate(POOL_WINDOWS):
        cols = slice(g * gw, (g + 1) * gw)
        win = p[:, cols]
        for j in range(1, w):
            win = win + ext_load(j, cols)
        cnt = jnp.minimum(pos + 1, w).astype(F32)
        feats.append(win / cnt - p[:, cols])
    return feats


def _pool_kernel(p_ref, w_ref, scale_ref, o_ref, buf_ref, ext_ref, *, s, gw):
    pad = BF16_ROWS
    ext_ref[0:pad, :] = jnp.zeros((pad, ext_ref.shape[1]), F32)
    ext_ref[pad:pad + s, :] = p_ref[...]
    p = p_ref[...]
    feats = _pool_features(lambda j, cols: ext_ref[pad - j:pad - j + s, cols], p, 0, gw)
    for g, f in enumerate(feats):
        cols = slice(g * gw, (g + 1) * gw)
        o = jnp.dot(f.astype(BF16), w_ref[g], preferred_element_type=F32) * scale_ref[:, cols]
        o_ref[:, cols] = o.astype(o_ref.dtype)
    buf_ref[...] = ext_ref[pad + s - POOL_BUF:pad + s, :]


def pool_mixer_prompt(z3, pool_w, pool_scale):
    b, s, _ = z3.shape
    ng, gw, _ = pool_w.shape
    width = ng * gw
    return pl.pallas_call(
        functools.partial(_pool_kernel, s=s, gw=gw),
        grid=(b,),
        in_specs=[pl.BlockSpec((None, s, width), lambda i: (i, 0, 0)),
                  pl.BlockSpec((ng, gw, gw), lambda i: (0, 0, 0)),
                  pl.BlockSpec((1, width), lambda i: (0, 0))],
        out_specs=[pl.BlockSpec((None, s, width), lambda i: (i, 0, 0)),
                   pl.BlockSpec((None, POOL_BUF, width), lambda i: (i, 0, 0))],
        out_shape=[jax.ShapeDtypeStruct((b, s, width), BF16),
                   jax.ShapeDtypeStruct((b, POOL_BUF, width), F32)],
        scratch_shapes=[pltpu.VMEM((BF16_ROWS + s, width), F32)],
        compiler_params=_params("parallel"),
        name="pool_mixer",
    )(z3, pool_w, pool_scale.reshape(1, width))


def _s5_discretize(a_re, a_im, log_dt, b_re, b_im, c_re, c_im):
    g, p = a_re.shape
    c = b_re.shape[-1]
    a = lax.complex(a_re.astype(F32), a_im.astype(F32))
    dt = jnp.exp(log_dt.astype(F32))[:, None]
    a_bar = jnp.exp(a * dt)
    b_bar = ((a_bar - 1.0) / a)[..., None] * lax.complex(b_re.astype(F32), b_im.astype(F32))
    gb = g // SSM_BLOCKS
    eye = jnp.eye(gb, dtype=F32)

    def b_blocks(x):
        x = x.reshape(SSM_BLOCKS, gb, p, c)
        return jnp.einsum('kgpc,gh->kgchp', x, eye).reshape(SSM_BLOCKS, gb * c, gb * p).astype(BF16)

    def c_blocks(x):
        x = x.reshape(SSM_BLOCKS, gb, c, p)
        return jnp.einsum('kgcp,gh->kgphc', x, eye).reshape(SSM_BLOCKS, gb * p, gb * c).astype(BF16)

    return (jnp.real(a_bar).reshape(1, g * p), jnp.imag(a_bar).reshape(1, g * p),
            b_blocks(jnp.real(b_bar)), b_blocks(jnp.imag(b_bar)),
            c_blocks(c_re.astype(F32)), c_blocks(-c_im.astype(F32)))


def _s5_input_proj(u_bf16, br_ref, bi_ref, store_re, store_im):
    cw, sw = br_ref.shape[1], br_ref.shape[2]
    for k in range(SSM_BLOCKS):
        uk = u_bf16[:, k * cw:(k + 1) * cw]
        store_re(k * sw, sw, jnp.dot(uk, br_ref[k], preferred_element_type=F32))
        store_im(k * sw, sw, jnp.dot(uk, bi_ref[k], preferred_element_type=F32))


def _s5_output(hs_re, hs_im, u, cr_ref, ci_ref, d_ref, gw_ref, gb_ref):
    sw, cw = cr_ref.shape[1], cr_ref.shape[2]
    ys = []
    for k in range(SSM_BLOCKS):
        ys.append(jnp.dot(hs_re(k * sw, sw).astype(BF16), cr_ref[k], preferred_element_type=F32)
                  + jnp.dot(hs_im(k * sw, sw).astype(BF16), ci_ref[k], preferred_element_type=F32))
    y = jnp.concatenate(ys, axis=-1) + d_ref[...] * u
    zz = jax.nn.gelu(y, approximate=True)
    gate = jnp.dot(zz.astype(BF16), gw_ref[...], preferred_element_type=F32) + gb_ref[...]
    return zz * jax.nn.sigmoid(gate)


def _s5_kernel(u_ref, h0r_ref, h0i_ref, ar_ref, ai_ref, br_ref, bi_ref, cr_ref, ci_ref, d_ref, gw_ref, gb_ref,
               o_ref, hr_ref, hi_ref, xr_sc, xi_sc, io_sc, *, nb, lane_chunk):
    rows, n_state = xr_sc.shape
    steps = SUBLANES // nb
    t_chunk = rows // nb
    n_slab = io_sc.shape[0]

    @pl.when(pl.program_id(0) == 0)
    def _():
        hr_ref[...] = h0r_ref[...]
        hi_ref[...] = h0i_ref[...]

    for b in range(nb):
        for c in range(n_slab):
            io_sc[c, pl.ds(b, t_chunk, stride=nb), :] = u_ref[b, :, c * LANES:(c + 1) * LANES]
    u = jnp.concatenate([io_sc[c] for c in range(n_slab)], axis=-1)

    def store(sc):
        def f(c0, w, val):
            sc[:, c0:c0 + w] = val
        return f

    _s5_input_proj(u.astype(BF16), br_ref, bi_ref, store(xr_sc), store(xi_sc))

    row = lax.broadcasted_iota(jnp.int32, (SUBLANES, lane_chunk), 0)
    for c0 in range(0, n_state, lane_chunk):
        cols = slice(c0, c0 + lane_chunk)
        ar = jnp.broadcast_to(ar_ref[:, cols], (SUBLANES, lane_chunk))
        ai = jnp.broadcast_to(ai_ref[:, cols], (SUBLANES, lane_chunk))

        def tile(i, carry, cols=cols, ar=ar, ai=ai):
            hr, hi = carry
            r0 = pl.multiple_of(i * SUBLANES, SUBLANES)
            xr = xr_sc[pl.ds(r0, SUBLANES), cols]
            xi = xi_sc[pl.ds(r0, SUBLANES), cols]
            out_r, out_i = xr, xi
            for k in range(steps):
                pr = pltpu.roll(hr, nb, 0) if nb < SUBLANES else hr
                pi = pltpu.roll(hi, nb, 0) if nb < SUBLANES else hi
                hr = ar * pr - ai * pi + xr
                hi = ar * pi + ai * pr + xi
                sel = (row >= k * nb) & (row < (k + 1) * nb)
                out_r = jnp.where(sel, hr, out_r)
                out_i = jnp.where(sel, hi, out_i)
            xr_sc[pl.ds(r0, SUBLANES), cols] = out_r
            xi_sc[pl.ds(r0, SUBLANES), cols] = out_i
            return hr, hi

        hr, hi = lax.fori_loop(0, rows // SUBLANES, tile, (hr_ref[:, cols], hi_ref[:, cols]))
        hr_ref[:, cols] = hr
        hi_ref[:, cols] = hi

    out = _s5_output(lambda c0, w: xr_sc[:, c0:c0 + w], lambda c0, w: xi_sc[:, c0:c0 + w],
                     u, cr_ref, ci_ref, d_ref, gw_ref, gb_ref)
    for c in range(n_slab):
        io_sc[c] = out[:, c * LANES:(c + 1) * LANES]
    for b in range(nb):
        for c in range(n_slab):
            o_ref[b, :, c * LANES:(c + 1) * LANES] = io_sc[c, pl.ds(b, t_chunk, stride=nb), :].astype(o_ref.dtype)


def s5_mixer_prompt(z3, u_col0, h0_re, h0_im, prm, *, t_chunk):
    nb, seq, _ = z3.shape
    a_re, a_im, b_re, b_im, c_re, c_im, d, glu_w, glu_b = prm
    n_state = a_re.shape[1]
    cw = d.shape[1]
    rows = t_chunk * nb
    const2 = lambda i: (0, 0)
    const3 = lambda i: (0, 0, 0)
    return pl.pallas_call(
        functools.partial(_s5_kernel, nb=nb, lane_chunk=512),
        grid=(seq // t_chunk,),
        in_specs=[pl.BlockSpec((nb, t_chunk, cw), lambda i: (0, i, u_col0 // cw)),
                  pl.BlockSpec((SUBLANES, n_state), const2), pl.BlockSpec((SUBLANES, n_state), const2),
                  pl.BlockSpec((1, n_state), const2), pl.BlockSpec((1, n_state), const2),
                  pl.BlockSpec(b_re.shape, const3), pl.BlockSpec(b_im.shape, const3),
                  pl.BlockSpec(c_re.shape, const3), pl.BlockSpec(c_im.shape, const3),
                  pl.BlockSpec((1, cw), const2), pl.BlockSpec((cw, cw), const2), pl.BlockSpec((1, cw), const2)],
        out_specs=[pl.BlockSpec((nb, t_chunk, cw), lambda i: (0, i, 0)),
                   pl.BlockSpec((SUBLANES, n_state), const2), pl.BlockSpec((SUBLANES, n_state), const2)],
        out_shape=[jax.ShapeDtypeStruct((nb, seq, cw), BF16),
                   jax.ShapeDtypeStruct((SUBLANES, n_state), F32),
                   jax.ShapeDtypeStruct((SUBLANES, n_state), F32)],
        scratch_shapes=[pltpu.VMEM((rows, n_state), F32), pltpu.VMEM((rows, n_state), F32),
                        pltpu.VMEM((cw // LANES, rows, LANES), F32)],
        compiler_params=_params("arbitrary"),
        name="s5_mixer",
    )(z3, h0_re, h0_im, a_re, a_im, b_re, b_im, c_re, c_im, d, glu_w, glu_b)


def _decode_mix_kernel(p_ref, u_ref, buf_ref, pw_ref, ps_ref, h0r_ref, h0i_ref, ar_ref, ai_ref,
                       br_ref, bi_ref, cr_ref, ci_ref, d_ref, gw_ref, gb_ref,
                       op_ref, nbuf_ref, os_ref, hr_ref, hi_ref, *, pos0, gw):
    p = p_ref[...]
    feats = []
    for g, w in enumerate(POOL_WINDOWS):
        cols = slice(g * gw, (g + 1) * gw)
        win = p[:, cols]
        for j in range(1, w):
            win = win + buf_ref[POOL_BUF - j, :, cols]
        feats.append(win / float(min(pos0 + 1, w)) - p[:, cols])
    for g, f in enumerate(feats):
        cols = slice(g * gw, (g + 1) * gw)
        o = jnp.dot(f.astype(BF16), pw_ref[g], preferred_element_type=F32) * ps_ref[:, cols]
        op_ref[:, cols] = o
    for r in range(POOL_BUF - 1):
        nbuf_ref[r] = buf_ref[r + 1]
    nbuf_ref[POOL_BUF - 1] = p

    u = u_ref[...]

    def store(ref):
        def f(c0, w, val):
            ref[:, c0:c0 + w] = val
        return f

    _s5_input_proj(u.astype(BF16), br_ref, bi_ref, store(hr_ref), store(hi_ref))
    h0r, h0i = h0r_ref[...], h0i_ref[...]
    ar, ai = ar_ref[...], ai_ref[...]
    hr = ar * h0r - ai * h0i + hr_ref[...]
    hi = ar * h0i + ai * h0r + hi_ref[...]
    hr_ref[...] = hr
    hi_ref[...] = hi
    os_ref[...] = _s5_output(lambda c0, w: hr_ref[:, c0:c0 + w], lambda c0, w: hi_ref[:, c0:c0 + w],
                             u, cr_ref, ci_ref, d_ref, gw_ref, gb_ref)


def decode_mixers(z, u_col0, buf_t, pool_w, pool_scale, h0_re, h0_im, s5prm, pos0):
    r = z.shape[0]
    ng, gw, _ = pool_w.shape
    pw = ng * gw
    a_re, a_im, b_re, b_im, c_re, c_im, d, glu_w, glu_b = s5prm
    cw = d.shape[1]
    n_state = a_re.shape[1]
    full = lambda a: pl.BlockSpec(a.shape, lambda i: (0,) * a.ndim)
    args = (buf_t, pool_w, pool_scale.reshape(1, pw), h0_re, h0_im, a_re, a_im, b_re, b_im, c_re, c_im,
            d, glu_w, glu_b)
    return pl.pallas_call(
        functools.partial(_decode_mix_kernel, pos0=pos0, gw=gw),
        grid=(1,),
        in_specs=[pl.BlockSpec((r, pw), lambda i: (0, 0)),
                  pl.BlockSpec((r, cw), lambda i: (0, u_col0 // cw))] + [full(a) for a in args],
        out_specs=[pl.BlockSpec((r, pw), lambda i: (0, 0)),
                   pl.BlockSpec((POOL_BUF, r, pw), lambda i: (0, 0, 0)),
                   pl.BlockSpec((r, cw), lambda i: (0, 0)),
                   pl.BlockSpec((r, n_state), lambda i: (0, 0)),
                   pl.BlockSpec((r, n_state), lambda i: (0, 0))],
        out_shape=[jax.ShapeDtypeStruct((r, pw), F32),
                   jax.ShapeDtypeStruct((POOL_BUF, r, pw), F32),
                   jax.ShapeDtypeStruct((r, cw), F32),
                   jax.ShapeDtypeStruct((r, n_state), F32),
                   jax.ShapeDtypeStruct((r, n_state), F32)],
        compiler_params=_params("arbitrary"),
        name="decode_mixers",
    )(z, z, *args)


def _lambda(lq1_ref, lk1_ref, lq2_ref, lk2_ref, lam_init):
    return (jnp.exp(jnp.sum(lq1_ref[...] * lk1_ref[...], axis=-1, keepdims=True))
            - jnp.exp(jnp.sum(lq2_ref[...] * lk2_ref[...], axis=-1, keepdims=True)) + lam_init)


def _diff_finish(o1, o2, lam, sw, lam_init):
    o = o1 - lam * o2
    return _rms(o, sw, SUBLN_EPS) * (1.0 - lam_init)


def _attn_kernel(q_ref, k_ref, v_ref, lq1_ref, lk1_ref, lq2_ref, lk2_ref, sw_ref, o_ref,
                 m_sc, l_sc, acc_sc, *, t, hd, scale, lam_init):
    i = pl.program_id(2)
    q = q_ref[...]
    c_exp = scale * math.log2(math.e)
    m_sc[...] = jnp.full(m_sc.shape, -jnp.inf, F32)
    l_sc[...] = jnp.zeros(l_sc.shape, F32)
    acc_sc[...] = jnp.zeros(acc_sc.shape, F32)

    def kv_tile(j, diagonal):
        r0 = pl.multiple_of(j * t, t)
        kt = k_ref[pl.ds(r0, t), :]
        vt = v_ref[pl.ds(r0, t), :]
        for mp in range(2):
            s = lax.dot_general(kt[:, mp * hd:(mp + 1) * hd], q[:, mp * hd:(mp + 1) * hd],
                                (((1,), (1,)), ((), ())), preferred_element_type=F32)
            if diagonal:
                kk = lax.broadcasted_iota(jnp.int32, (t, t), 0)
                qq = lax.broadcasted_iota(jnp.int32, (t, t), 1)
                s = jnp.where(kk <= qq, s, -jnp.inf)
            m_old = m_sc[mp]
            m_new = jnp.maximum(m_old, jnp.max(s, axis=0, keepdims=True))
            alpha = jnp.exp2((m_old - m_new) * c_exp)
            p = jnp.exp2((s - m_new) * c_exp)
            l_sc[mp] = alpha * l_sc[mp] + jnp.sum(p, axis=0, keepdims=True)
            pv = lax.dot_general(vt, p.astype(BF16), (((0,), (0,)), ((), ())), preferred_element_type=F32)
            acc_sc[mp] = alpha * acc_sc[mp] + pv
            m_sc[mp] = m_new

    def body(j, c):
        kv_tile(j, False)
        return c

    lax.fori_loop(0, i, body, 0)
    kv_tile(i, True)
    lam = _lambda(lq1_ref, lk1_ref, lq2_ref, lk2_ref, lam_init)
    o = acc_sc[0] / l_sc[0] - lam * (acc_sc[1] / l_sc[1])
    o = o * lax.rsqrt(jnp.mean(o * o, axis=0, keepdims=True) + SUBLN_EPS) * sw_ref[...] * (1.0 - lam_init)
    o_ref[...] = o.T.astype(o_ref.dtype)


def diff_attention_prompt(q, k, v, lam_prm, subln_w, *, batch, lam_init, t):
    m, width = q.shape
    s = m // batch
    hd = lam_prm[0].shape[-1]
    vd = 2 * hd
    nh = width // vd
    t = _tile(s, t)
    nq = s // t
    vec = lambda n: pl.BlockSpec((1, n), lambda b, h, i: (0, 0))
    return pl.pallas_call(
        functools.partial(_attn_kernel, t=t, hd=hd, scale=hd ** -0.5, lam_init=lam_init),
        grid=(batch, nh, nq),
        in_specs=[pl.BlockSpec((t, vd), lambda b, h, i: (b * nq + i, h)),
                  pl.BlockSpec((s, vd), lambda b, h, i: (b, h)),
                  pl.BlockSpec((s, vd), lambda b, h, i: (b, h)),
                  vec(hd), vec(hd), vec(hd), vec(hd), pl.BlockSpec((vd, 1), lambda b, h, i: (0, 0))],
        out_specs=pl.BlockSpec((t, vd), lambda b, h, i: (b * nq + i, h)),
        out_shape=jax.ShapeDtypeStruct((m, width), BF16),
        scratch_shapes=[pltpu.VMEM((2, 1, t), F32), pltpu.VMEM((2, 1, t), F32), pltpu.VMEM((2, vd, t), F32)],
        compiler_params=_params("parallel", "parallel", "arbitrary"),
        name="diff_attention",
    )(q, k, v, *[p.reshape(1, hd) for p in lam_prm], subln_w.reshape(vd, 1))


def _dec_attn_kernel(pt_ref, q_ref, *refs, n_pp, c_exp, lam_init):
    kc_refs, vc_refs = refs[:n_pp], refs[n_pp:2 * n_pp]
    kn_ref, vn_ref, lq1_ref, lk1_ref, lq2_ref, lk2_ref, sw_ref, o_ref, m_sc, l_sc, acc_sc = refs[2 * n_pp:]
    pg = pl.program_id(1)
    nh, hd = m_sc.shape[1], m_sc.shape[2]
    ones = jnp.ones((hd, hd), BF16)
    twice = lambda x: jnp.concatenate([x, x], axis=-1)

    @pl.when(pg == 0)
    def _():
        m_sc[...] = jnp.full(m_sc.shape, -jnp.inf, F32)
        l_sc[...] = jnp.zeros(l_sc.shape, F32)
        acc_sc[...] = jnp.zeros(acc_sc.shape, F32)

    def update(keys, values):
        for mp in range(2):
            qm = q_ref[pl.ds(mp, nh, stride=2), :]
            s = []
            for k in keys:
                prod = k(mp) * qm[None]
                rows = prod.shape[0]
                rep = jnp.dot(prod.reshape(rows * nh, hd).astype(BF16), ones, preferred_element_type=F32)
                s.append(rep.reshape(rows, nh, hd))
            m_old = m_sc[mp]
            m_new = functools.reduce(jnp.maximum, [jnp.max(x, axis=0) for x in s], m_old)
            alpha = jnp.exp2((m_old - m_new) * c_exp)
            p = [jnp.exp2((x - m_new) * c_exp) for x in s]
            l_sc[mp] = alpha * l_sc[mp] + sum(jnp.sum(x, axis=0) for x in p)
            acc_sc[mp] = twice(alpha) * acc_sc[mp] + sum(jnp.sum(twice(x) * v, axis=0) for x, v in zip(p, values))
            m_sc[mp] = m_new

    update([lambda mp, r=r: r[:, pl.ds(mp, nh, stride=2), :] for r in kc_refs], [r[...] for r in vc_refs])

    @pl.when(pg == pl.num_programs(1) - 1)
    def _():
        update([lambda mp: kn_ref[pl.ds(mp, nh, stride=2), :][None]], [vn_ref[...][None]])
        lam = _lambda(lq1_ref, lk1_ref, lq2_ref, lk2_ref, lam_init)
        o_ref[...] = _diff_finish(acc_sc[0] / twice(l_sc[0]), acc_sc[1] / twice(l_sc[1]), lam, sw_ref[...],
                                  lam_init)


def diff_attention_decode(q, k_new, v_new, cache_k, cache_v, layer, page_table, lam_prm, subln_w, *, lam_init, n_pp):
    r, nj, hd = q.shape
    nh, vd = v_new.shape[1], v_new.shape[2]
    n_pages = page_table.shape[1]
    page = cache_k.shape[2]
    assert n_pages % n_pp == 0
    vec = lambda n: pl.BlockSpec((1, n), lambda b, p, pt: (0, 0))
    row_map = lambda b, p, pt: (b, 0, 0)

    def page_spec(heads, dim, t):
        return pl.BlockSpec((None, None, page, heads, dim),
                            lambda b, p, pt: (layer, pt[b * n_pages + p * n_pp + t], 0, 0, 0))

    grid_spec = pltpu.PrefetchScalarGridSpec(
        num_scalar_prefetch=1,
        grid=(r, n_pages // n_pp),
        in_specs=[pl.BlockSpec((None, nj, hd), row_map)]
        + [page_spec(nj, hd, t) for t in range(n_pp)] + [page_spec(nh, vd, t) for t in range(n_pp)]
        + [pl.BlockSpec((None, nj, hd), row_map), pl.BlockSpec((None, nh, vd), row_map),
           vec(hd), vec(hd), vec(hd), vec(hd), vec(vd)],
        out_specs=pl.BlockSpec((None, nh, vd), row_map),
        scratch_shapes=[pltpu.VMEM((2, nh, hd), F32), pltpu.VMEM((2, nh, hd), F32), pltpu.VMEM((2, nh, vd), F32)],
    )
    return pl.pallas_call(
        functools.partial(_dec_attn_kernel, n_pp=n_pp, c_exp=hd ** -0.5 * math.log2(math.e), lam_init=lam_init),
        grid_spec=grid_spec,
        out_shape=jax.ShapeDtypeStruct((r, nh, vd), F32),
        compiler_params=_params("parallel", "arbitrary"),
        name="diff_attention_decode",
    )(page_table.reshape(-1), q, *([cache_k] * n_pp), *([cache_v] * n_pp), k_new, v_new,
      *[p.reshape(1, hd) for p in lam_prm], subln_w.reshape(1, vd))


def kernel(x_prompt, x_sample, cache_k, cache_v, state_pool, state_ssm_re, state_ssm_im, page_table, norm_mix_pre, norm_mix_post, norm_ffn_pre, norm_ffn_post, w_in, pool_w, pool_scale, lambda_q1, lambda_k1, lambda_q2, lambda_k2, subln_w, ssm_a_re, ssm_a_im, ssm_b_re, ssm_b_im, ssm_c_re, ssm_c_im, ssm_d, ssm_log_dt, glu_w, glu_b, w_lift, w_out, ffn_w_gate, ffn_w_up, ffn_w_down):
    bp, seq, d_model = x_prompt.shape
    db, dseq, _ = x_sample.shape
    assert dseq == 1
    depth = w_in.shape[0]
    page = cache_k.shape[2]
    n_past = page_table.shape[1] * page
    n_heads2, hd = cache_k.shape[3], cache_k.shape[4]
    n_heads, vd = cache_v.shape[3], cache_v.shape[4]
    attn_qw = n_heads2 * hd
    attn_vw = n_heads * vd
    pool_width = pool_w.shape[1] * pool_w.shape[2]
    ssm_w = ssm_d.shape[1]
    n_groups, n_st = ssm_a_re.shape[1], ssm_a_re.shape[2]
    n_state = n_groups * n_st
    d_ff = ffn_w_gate.shape[-1]
    in_width = w_in.shape[2]
    q_col0 = pool_width
    u_col0 = pool_width + 2 * attn_qw + attn_vw
    gate_col0 = u_col0 + ssm_w
    tn = 512
    assert SUBLANES % bp == 0 and attn_qw == attn_vw and u_col0 % ssm_w == 0 and ssm_w % LANES == 0

    mp = bp * seq
    rs = BF16_ROWS
    xp = x_prompt.reshape(mp, d_model)
    xs = jnp.pad(x_sample.reshape(db, d_model), ((0, rs - db), (0, 0)))
    pad_s = lambda a: jnp.pad(a, ((0, rs - db),) + ((0, 0),) * (a.ndim - 1))

    cos_p, sin_p = _rope_tables(jnp.arange(seq, dtype=jnp.int32), hd)
    cos_s, sin_s = _rope_tables(jnp.full((rs,), n_past, jnp.int32), hd)

    hp = rmsnorm_bf16(xp, norm_mix_pre[0])
    hs = rmsnorm_bf16(xs, norm_mix_pre[0])
    outs = {k: [] for k in ("pp", "rp", "ip", "ks", "vs", "ps", "rs", "is")}
    t_seq = _tile(seq, 1024)
    t_lift = _tile(seq, 512)
    ident = lambda j, i: (i, j)
    k_all = v_all = None

    for l in range(depth):
        lam_init = 0.8 - 0.6 * math.exp(-0.3 * l)
        a_re, a_im, b_re, b_im, c_re, c_im = _s5_discretize(
            ssm_a_re[l], ssm_a_im[l], ssm_log_dt[l], ssm_b_re[l], ssm_b_im[l], ssm_c_re[l], ssm_c_im[l])
        s5prm = (a_re, a_im, b_re, b_im, c_re, c_im, ssm_d[l].reshape(1, ssm_w), glu_w[l].astype(BF16),
                 glu_b[l].reshape(1, ssm_w))
        lam_prm = (lambda_q1[l], lambda_k1[l], lambda_q2[l], lambda_k2[l])
        w_next = norm_mix_pre[l + 1] if l + 1 < depth else None
        pool_w_l = pool_w[l].astype(BF16)

        zs, z = project(hs, hp, w_in, l, tm=t_seq, tn=tn, n_blocks=in_width // tn, w_col=lambda j: j,
                        s_shape=(rs, in_width), s_map=lambda j: (0, j), p_shape=(mp, in_width), p_map=ident)
        z3 = z.reshape(bp, seq, in_width)

        q, k, k_all, v, v_all = rope_split(z, cos_p, sin_p, q_col0=q_col0, width=attn_qw, head_dim=hd, tm=512,
                                           stack=(l, depth, vd, k_all, v_all))
        o_pool, pbuf = pool_mixer_prompt(z3, pool_w_l, pool_scale[l])
        o_attn = diff_attention_prompt(q, k, v, lam_prm, subln_w[l], batch=bp, lam_init=lam_init, t=ATTN_TILE)
        zero_h = jnp.zeros((SUBLANES, n_state), F32)
        o_ssm, hre, him = s5_mixer_prompt(z3, u_col0, zero_h, zero_h, s5prm, t_chunk=S5_TIME_CHUNK)

        qs, _, kfs, _, vfs = rope_split(zs, cos_s, sin_s, q_col0=q_col0, width=attn_qw, head_dim=hd, tm=rs)
        buf_t = pad_s(state_pool[l]).transpose(1, 0, 2)
        o_pool_s, nbuf_t, o_ssm_s, sre, sim = decode_mixers(
            zs, u_col0, buf_t, pool_w_l, pool_scale[l],
            pad_s(state_ssm_re[l].reshape(db, n_state)), pad_s(state_ssm_im[l].reshape(db, n_state)),
            s5prm, n_past)
        o_attn_s = diff_attention_decode(
            qs[:db].astype(F32).reshape(db, n_heads2, hd), kfs[:db].reshape(db, n_heads2, hd),
            vfs[:db].reshape(db, n_heads, vd), cache_k, cache_v, l, page_table, lam_prm, subln_w[l],
            lam_init=lam_init, n_pp=math.gcd(DECODE_PAGES_PER_STEP, page_table.shape[1]))

        branches_s = (o_pool_s.astype(BF16), pad_s(o_attn_s.reshape(db, attn_vw)).astype(BF16), o_ssm_s.astype(BF16))
        branches_p = (o_pool.reshape(mp, pool_width), o_attn, o_ssm.reshape(mp, ssm_w))
        merged_s, merged = lift_merge(branches_s, zs, branches_p, z, lambda i: (i, 0),
                                      w_lift, l, gate_col0, tm=t_lift, tn=tn)
        ys, y = project(merged_s, merged, w_out, l, tm=t_seq, tn=tn, n_blocks=d_model // tn, w_col=lambda j: j,
                        s_shape=(rs, d_model), s_map=lambda j: (0, j), p_shape=(mp, d_model), p_map=ident)
        xs, h2s = residual_norm(xs, ys, norm_mix_post[l], norm_ffn_pre[l])
        xp, h2 = residual_norm(xp, y, norm_mix_post[l], norm_ffn_pre[l])
        act_s, act = swiglu(h2s, h2, ffn_w_gate, ffn_w_up, l, tm=1024, tn=256)
        w_down = ffn_w_down[l].astype(BF16)
        fs = matmul_ksplit(act_s, w_down, tm=rs, tn=512, tk=d_ff // 2)
        f = matmul_ksplit(act, w_down, tm=1024, tn=512, tk=d_ff // 2)
        xs, hs = residual_norm(xs, fs, norm_ffn_post[l], w_next)
        xp, hp = residual_norm(xp, f, norm_ffn_post[l], w_next)

        outs["pp"].append(pbuf)
        outs["rp"].append(hre[SUBLANES - bp:].reshape(bp, n_groups, n_st))
        outs["ip"].append(him[SUBLANES - bp:].reshape(bp, n_groups, n_st))
        outs["ks"].append(kfs[:db].reshape(db, 1, n_heads2, hd))
        outs["vs"].append(vfs[:db].reshape(db, 1, n_heads, vd))
        outs["ps"].append(nbuf_t.transpose(1, 0, 2)[:db])
        outs["rs"].append(sre[:db].reshape(db, n_groups, n_st))
        outs["is"].append(sim[:db].reshape(db, n_groups, n_st))

    st = lambda key: jnp.stack(outs[key])
    return (xp.reshape(bp, seq, d_model), xs[:db].reshape(db, 1, d_model),
            k_all.reshape(depth, bp, seq, n_heads2, hd), v_all.reshape(depth, bp, seq, n_heads, vd),
            st("pp"), st("rp"), st("ip"),
            st("ks"), st("vs"), st("ps"), st("rs"), st("is"))
```

```python
import functools
import math

import jax
import jax.numpy as jnp
from jax import lax
from jax.experimental import pallas as pl
from jax.experimental.pallas import tpu as pltpu

F32 = jnp.float32
BF16 = jnp.bfloat16

V7X_VMEM_BYTES = 64 * 1024 * 1024
VMEM_LIMIT_BYTES = V7X_VMEM_BYTES - 8 * 1024 * 1024
SUBLANES = 8
LANES = 128
BF16_ROWS = 16

NORM_EPS = 1e-6
SUBLN_EPS = 1e-5
ROPE_THETA = 10000.0
POOL_WINDOWS = (2, 4, 8, 16)
POOL_BUF = max(POOL_WINDOWS) - 1
SSM_BLOCKS = 4
ATTN_TILE = 1024
S5_TIME_CHUNK = 64
DECODE_PAGES_PER_STEP = 8
CAST_CHUNK = 1024


def _params(*sem):
    return pltpu.CompilerParams(dimension_semantics=sem, vmem_limit_bytes=VMEM_LIMIT_BYTES)


def _tile(n, pref):
    t = min(n, pref)
    while n % t:
        t //= 2
    return t


def _rms(x, w, eps):
    return x * lax.rsqrt(jnp.mean(x * x, axis=-1, keepdims=True) + eps) * w


def _rmsnorm_kernel(x_ref, w_ref, o_ref):
    o_ref[...] = _rms(x_ref[...], w_ref[...], NORM_EPS).astype(o_ref.dtype)


def rmsnorm_bf16(x, w):
    m, d = x.shape
    tm = _tile(m, 256)
    return pl.pallas_call(
        _rmsnorm_kernel,
        grid=(m // tm,),
        in_specs=[pl.BlockSpec((tm, d), lambda i: (i, 0)), pl.BlockSpec((1, d), lambda i: (0, 0))],
        out_specs=pl.BlockSpec((tm, d), lambda i: (i, 0)),
        out_shape=jax.ShapeDtypeStruct((m, d), BF16),
        compiler_params=_params("parallel"),
        name="rmsnorm",
    )(x, w.reshape(1, d))


def _resnorm_kernel(x_ref, y_ref, wpost_ref, wnext_ref, xo_ref, ho_ref):
    x = x_ref[...] + _rms(y_ref[...], wpost_ref[...], NORM_EPS)
    xo_ref[...] = x
    ho_ref[...] = _rms(x, wnext_ref[...], NORM_EPS).astype(ho_ref.dtype)


def _resnorm_last_kernel(x_ref, y_ref, wpost_ref, xo_ref):
    xo_ref[...] = x_ref[...] + _rms(y_ref[...], wpost_ref[...], NORM_EPS)


def residual_norm(x, y, w_post, w_next):
    m, d = x.shape
    tm = _tile(m, 256)
    row = pl.BlockSpec((tm, d), lambda i: (i, 0))
    vec = pl.BlockSpec((1, d), lambda i: (0, 0))
    if w_next is None:
        return pl.pallas_call(
            _resnorm_last_kernel, grid=(m // tm,), in_specs=[row, row, vec], out_specs=row,
            out_shape=jax.ShapeDtypeStruct((m, d), F32), compiler_params=_params("parallel"),
            name="resnorm_last",
        )(x, y, w_post.reshape(1, d)), None
    return pl.pallas_call(
        _resnorm_kernel, grid=(m // tm,), in_specs=[row, row, vec, vec], out_specs=[row, row],
        out_shape=[jax.ShapeDtypeStruct((m, d), F32), jax.ShapeDtypeStruct((m, d), BF16)],
        compiler_params=_params("parallel"), name="resnorm",
    )(x, y, w_post.reshape(1, d), w_next.reshape(1, d))


def _first_row_step():
    return pl.program_id(1) == 0


def _cast_chunks(w_ref, wbf_ref):
    k = w_ref.shape[0]
    kc = _tile(k, CAST_CHUNK)
    for c0 in range(0, k, kc):
        wb = w_ref[c0:c0 + kc, :].astype(BF16)
        wbf_ref[c0:c0 + kc, :] = wb
        yield slice(c0, c0 + kc), wb


def _proj_kernel(xs_ref, xp_ref, w_ref, os_ref, op_ref, wbf_ref, *, gate):
    post = jax.nn.sigmoid if gate else (lambda x: x)

    @pl.when(_first_row_step())
    def _():
        acc_s = acc_p = 0.0
        for rows, wb in _cast_chunks(w_ref, wbf_ref):
            acc_p = acc_p + jnp.dot(xp_ref[:, rows], wb, preferred_element_type=F32)
            acc_s = acc_s + jnp.dot(xs_ref[:, rows], wb, preferred_element_type=F32)
        os_ref[...] = post(acc_s).astype(os_ref.dtype)
        op_ref[...] = post(acc_p).astype(op_ref.dtype)

    @pl.when(jnp.logical_not(_first_row_step()))
    def _():
        op_ref[...] = post(jnp.dot(xp_ref[...], wbf_ref[...], preferred_element_type=F32)).astype(op_ref.dtype)


def project(xs, xp, w, layer, *, tm, tn, n_blocks, w_col, s_shape, s_map, p_shape, p_map, out_dtype=F32,
            gate=False):
    rs, k = xs.shape
    tm = _tile(xp.shape[0], tm)
    return pl.pallas_call(
        functools.partial(_proj_kernel, gate=gate),
        grid=(n_blocks, xp.shape[0] // tm),
        in_specs=[pl.BlockSpec((rs, k), lambda j, i: (0, 0)),
                  pl.BlockSpec((tm, k), lambda j, i: (i, 0)),
                  pl.BlockSpec((None, k, tn), lambda j, i: (layer, 0, w_col(j)))],
        out_specs=[pl.BlockSpec((rs, tn), lambda j, i: s_map(j)),
                   pl.BlockSpec((tm, tn), p_map)],
        out_shape=[jax.ShapeDtypeStruct(s_shape, out_dtype), jax.ShapeDtypeStruct(p_shape, out_dtype)],
        scratch_shapes=[pltpu.VMEM((k, tn), BF16)],
        compiler_params=_params("parallel", "arbitrary"),
        name="project",
    )(xs, xp, w)


def _silu_mul(g, u):
    return g * jax.nn.sigmoid(g) * u


def _swiglu_kernel(xs_ref, xp_ref, wg_ref, wu_ref, os_ref, op_ref, wgbf_ref, wubf_ref):
    @pl.when(_first_row_step())
    def _():
        gs = gp = us = up = 0.0
        for (rows, wg), (_, wu) in zip(_cast_chunks(wg_ref, wgbf_ref), _cast_chunks(wu_ref, wubf_ref)):
            xp, xs = xp_ref[:, rows], xs_ref[:, rows]
            gp = gp + jnp.dot(xp, wg, preferred_element_type=F32)
            up = up + jnp.dot(xp, wu, preferred_element_type=F32)
            gs = gs + jnp.dot(xs, wg, preferred_element_type=F32)
            us = us + jnp.dot(xs, wu, preferred_element_type=F32)
        os_ref[...] = _silu_mul(gs, us).astype(os_ref.dtype)
        op_ref[...] = _silu_mul(gp, up).astype(op_ref.dtype)

    @pl.when(jnp.logical_not(_first_row_step()))
    def _():
        x = xp_ref[...]
        g = jnp.dot(x, wgbf_ref[...], preferred_element_type=F32)
        u = jnp.dot(x, wubf_ref[...], preferred_element_type=F32)
        op_ref[...] = _silu_mul(g, u).astype(op_ref.dtype)


def swiglu(xs, xp, wg, wu, layer, *, tm, tn):
    rs, k = xs.shape
    m = xp.shape[0]
    n = wg.shape[2]
    tm = _tile(m, tm)
    wspec = pl.BlockSpec((None, k, tn), lambda j, i: (layer, 0, j))
    return pl.pallas_call(
        _swiglu_kernel,
        grid=(n // tn, m // tm),
        in_specs=[pl.BlockSpec((rs, k), lambda j, i: (0, 0)),
                  pl.BlockSpec((tm, k), lambda j, i: (i, 0)), wspec, wspec],
        out_specs=[pl.BlockSpec((rs, tn), lambda j, i: (0, j)),
                   pl.BlockSpec((tm, tn), lambda j, i: (i, j))],
        out_shape=[jax.ShapeDtypeStruct((rs, n), BF16), jax.ShapeDtypeStruct((m, n), BF16)],
        scratch_shapes=[pltpu.VMEM((k, tn), BF16), pltpu.VMEM((k, tn), BF16)],
        compiler_params=_params("parallel", "arbitrary"),
        name="swiglu",
    )(xs, xp, wg, wu)


def _merge(o_pool_ref, o_attn_ref, o_ssm_ref, g0_ref, g1_ref, g2_ref, wbf_ref):
    pw, aw = o_pool_ref.shape[1], o_attn_ref.shape[1]
    up = jnp.dot(o_pool_ref[...], wbf_ref[0:pw, :], preferred_element_type=F32)
    ua = jnp.dot(o_attn_ref[...], wbf_ref[pw:pw + aw, :], preferred_element_type=F32)
    us = jnp.dot(o_ssm_ref[...], wbf_ref[pw + aw:, :], preferred_element_type=F32)
    return g0_ref[...] * up + g1_ref[...] * ua + g2_ref[...] * us


def _lift_kernel(*refs):
    s_in, p_in, (w_ref, os_ref, op_ref, wbf_ref) = refs[0:6], refs[6:12], refs[12:]

    @pl.when(_first_row_step())
    def _():
        wbf_ref[...] = w_ref[...].astype(BF16)
        os_ref[...] = _merge(*s_in, wbf_ref).astype(os_ref.dtype)

    op_ref[...] = _merge(*p_in, wbf_ref).astype(op_ref.dtype)


def lift_merge(branches_s, gs, branches_p, gp, w_lift, layer, *, tm, tn):
    rs = gs.shape[0]
    m = gp.shape[0]
    k, d = w_lift.shape[1], w_lift.shape[2]
    tm = _tile(m, tm)
    gb, nb = 0, d // tn
    widths = [o.shape[1] for o in branches_s]
    s_specs = [pl.BlockSpec((rs, wd), lambda j, i: (0, 0)) for wd in widths]
    s_gates = [pl.BlockSpec((rs, tn), lambda j, i, br=br: (0, gb + br * nb + j)) for br in range(3)]
    p_specs = [pl.BlockSpec((tm, widths[0]), lambda j, i: (i, 0)),
               pl.BlockSpec((tm, widths[1]), lambda j, i: (i, 0)),
               pl.BlockSpec((tm, widths[2]), lambda j, i: (i, 0))]
    p_gates = [pl.BlockSpec((tm, tn), lambda j, i, br=br: (i, gb + br * nb + j)) for br in range(3)]
    return pl.pallas_call(
        _lift_kernel,
        grid=(nb, m // tm),
        in_specs=s_specs + s_gates + p_specs + p_gates
        + [pl.BlockSpec((None, k, tn), lambda j, i: (layer, 0, j))],
        out_specs=[pl.BlockSpec((rs, tn), lambda j, i: (0, j)), pl.BlockSpec((tm, tn), lambda j, i: (i, j))],
        out_shape=[jax.ShapeDtypeStruct((rs, d), BF16), jax.ShapeDtypeStruct((m, d), BF16)],
        scratch_shapes=[pltpu.VMEM((k, tn), BF16)],
        compiler_params=_params("parallel", "arbitrary"),
        name="lift_merge",
    )(*branches_s, gs, gs, gs, *branches_p, gp, gp, gp, w_lift)


def _mm_acc_kernel(x_ref, w_ref, o_ref, acc_ref):
    kk = pl.program_id(2)

    @pl.when(kk == 0)
    def _():
        acc_ref[...] = jnp.zeros_like(acc_ref)

    acc_ref[...] += jnp.dot(x_ref[...], w_ref[...], preferred_element_type=F32)

    @pl.when(kk == pl.num_programs(2) - 1)
    def _():
        o_ref[...] = acc_ref[...]


def matmul_ksplit(x, w, *, tm, tn, tk):
    m, k = x.shape
    n = w.shape[1]
    tm = _tile(m, tm)
    return pl.pallas_call(
        _mm_acc_kernel,
        grid=(m // tm, n // tn, k // tk),
        in_specs=[pl.BlockSpec((tm, tk), lambda i, j, kk: (i, kk)),
                  pl.BlockSpec((tk, tn), lambda i, j, kk: (kk, j))],
        out_specs=pl.BlockSpec((tm, tn), lambda i, j, kk: (i, j)),
        out_shape=jax.ShapeDtypeStruct((m, n), F32),
        scratch_shapes=[pltpu.VMEM((tm, tn), F32)],
        compiler_params=_params("parallel", "parallel", "arbitrary"),
        name="matmul_ksplit",
    )(x, w)


def _rope_tables(pos, head_dim):
    half = head_dim // 2
    inv = ROPE_THETA ** (-jnp.arange(half, dtype=F32) / half)
    ang = pos.astype(F32)[:, None] * inv[None, :]
    cos, sin = jnp.cos(ang), jnp.sin(ang)
    return jnp.concatenate([cos, cos], axis=-1), jnp.concatenate([-sin, sin], axis=-1)


def _rope_kernel(q_ref, k_ref, v_ref, cos_ref, sin_ref, *rest, head_dim, stacked):
    qo_ref, ko_ref, kf_ref, vo_ref, vf_ref = rest[-5:]
    cos, sin = cos_ref[...], sin_ref[...]
    n_heads = q_ref.shape[1] // head_dim

    def rot(x):
        return x * cos + pltpu.roll(x, head_dim // 2, 1) * sin

    for h in range(n_heads):
        cols = slice(h * head_dim, (h + 1) * head_dim)
        qo_ref[:, cols] = rot(q_ref[:, cols]).astype(qo_ref.dtype)
        r = rot(k_ref[:, cols])
        ko_ref[:, cols] = r.astype(ko_ref.dtype)
        if stacked:
            kf_ref[:, h, :] = r
        else:
            kf_ref[:, cols] = r
    v = v_ref[...]
    vo_ref[...] = v.astype(vo_ref.dtype)
    if stacked:
        vd = vf_ref.shape[2]
        v_heads = v.shape[1] // vd
        h0 = pl.program_id(1) * v_heads
        for h in range(v_heads):
            vf_ref[:, pl.ds(h0 + h, 1), :] = v[:, None, h * vd:(h + 1) * vd]
    else:
        vf_ref[...] = v


def rope_split(z, cos, sin, *, q_col0, width, head_dim, tm, stack=None):
    m = z.shape[0]
    s_tab = cos.shape[0]
    tm = _tile(min(m, s_tab), tm)
    tc = 1024
    nc = width // tc
    qb = q_col0 // tc
    n_tab = s_tab // tm

    def zspec(off):
        return pl.BlockSpec((tm, tc), lambda i, j: (i, off + j))

    tab = pl.BlockSpec((tm, head_dim), lambda i, j: (i % n_tab, 0))
    out = pl.BlockSpec((tm, tc), lambda i, j: (i, j))
    in_specs = [zspec(qb), zspec(qb + nc), zspec(qb + 2 * nc), tab, tab]
    args = [z, z, z, cos, sin]
    aliases = {}
    if stack is None:
        f32_specs = [out, out]
        f32_shapes = [jax.ShapeDtypeStruct((m, width), F32)] * 2
    else:
        layer, depth, vd, k_all, v_all = stack
        kh, vh = tc // head_dim, width // vd
        f32_specs = [pl.BlockSpec((None, tm, kh, head_dim), lambda i, j: (layer, i, j, 0)),
                     pl.BlockSpec((None, tm, vh, vd), lambda i, j: (layer, i, 0, 0))]
        f32_shapes = [jax.ShapeDtypeStruct((depth, m, width // head_dim, head_dim), F32),
                      jax.ShapeDtypeStruct((depth, m, vh, vd), F32)]
        if k_all is not None:
            in_specs += [pl.BlockSpec(memory_space=pl.ANY)] * 2
            args += [k_all, v_all]
            aliases = {5: 2, 6: 4}
    return pl.pallas_call(
        functools.partial(_rope_kernel, head_dim=head_dim, stacked=stack is not None),
        grid=(m // tm, nc),
        in_specs=in_specs,
        out_specs=[out, out, f32_specs[0], out, f32_specs[1]],
        out_shape=[jax.ShapeDtypeStruct((m, width), BF16), jax.ShapeDtypeStruct((m, width), BF16), f32_shapes[0],
                   jax.ShapeDtypeStruct((m, width), BF16), f32_shapes[1]],
        input_output_aliases=aliases,
        compiler_params=_params("parallel", "arbitrary"),
        name="rope_split",
    )(*args)


def _pool_features(ext_load, p, pos0, gw):
    s = p.shape[0]
    pos = pos0 + lax.broadcasted_iota(jnp.int32, (s, 1), 0)
    feats = []
    for g, w in enumerate(POOL_WINDOWS):
        cols = slice(g * gw, (g + 1) * gw)
        win = p[:, cols]
        for j in range(1, w):
            win = win + ext_load(j, cols)
        cnt = jnp.minimum(pos + 1, w).astype(F32)
        feats.append(win / cnt - p[:, cols])
    return feats


def _pool_kernel(p_ref, w_ref, scale_ref, o_ref, buf_ref, ext_ref, *, s, gw):
    pad = BF16_ROWS
    ext_ref[0:pad, :] = jnp.zeros((pad, ext_ref.shape[1]), F32)
    ext_ref[pad:pad + s, :] = p_ref[...]
    p = p_ref[...]
    feats = _pool_features(lambda j, cols: ext_ref[pad - j:pad - j + s, cols], p, 0, gw)
    for g, f in enumerate(feats):
        cols = slice(g * gw, (g + 1) * gw)
        o = jnp.dot(f.astype(BF16), w_ref[g], preferred_element_type=F32) * scale_ref[:, cols]
        o_ref[:, cols] = o.astype(o_ref.dtype)
    buf_ref[...] = ext_ref[pad + s - POOL_BUF:pad + s, :]


def pool_mixer_prompt(z3, pool_w, pool_scale):
    b, s, _ = z3.shape
    ng, gw, _ = pool_w.shape
    width = ng * gw
    return pl.pallas_call(
        functools.partial(_pool_kernel, s=s, gw=gw),
        grid=(b,),
        in_specs=[pl.BlockSpec((None, s, width), lambda i: (i, 0, 0)),
                  pl.BlockSpec((ng, gw, gw), lambda i: (0, 0, 0)),
                  pl.BlockSpec((1, width), lambda i: (0, 0))],
        out_specs=[pl.BlockSpec((None, s, width), lambda i: (i, 0, 0)),
                   pl.BlockSpec((None, POOL_BUF, width), lambda i: (i, 0, 0))],
        out_shape=[jax.ShapeDtypeStruct((b, s, width), BF16),
                   jax.ShapeDtypeStruct((b, POOL_BUF, width), F32)],
        scratch_shapes=[pltpu.VMEM((BF16_ROWS + s, width), F32)],
        compiler_params=_params("parallel"),
        name="pool_mixer",
    )(z3, pool_w, pool_scale.reshape(1, width))


def _s5_discretize(a_re, a_im, log_dt, b_re, b_im, c_re, c_im):
    g, p = a_re.shape
    c = b_re.shape[-1]
    a = lax.complex(a_re.astype(F32), a_im.astype(F32))
    dt = jnp.exp(log_dt.astype(F32))[:, None]
    a_bar = jnp.exp(a * dt)
    b_bar = ((a_bar - 1.0) / a)[..., None] * lax.complex(b_re.astype(F32), b_im.astype(F32))
    gb = g // SSM_BLOCKS
    eye = jnp.eye(gb, dtype=F32)

    def b_blocks(x):
        x = x.reshape(SSM_BLOCKS, gb, p, c)
        return jnp.einsum('kgpc,gh->kgchp', x, eye).reshape(SSM_BLOCKS, gb * c, gb * p).astype(BF16)

    def c_blocks(x):
        x = x.reshape(SSM_BLOCKS, gb, c, p)
        return jnp.einsum('kgcp,gh->kgphc', x, eye).reshape(SSM_BLOCKS, gb * p, gb * c).astype(BF16)

    return (jnp.real(a_bar).reshape(1, g * p), jnp.imag(a_bar).reshape(1, g * p),
            b_blocks(jnp.real(b_bar)), b_blocks(jnp.imag(b_bar)),
            c_blocks(c_re.astype(F32)), c_blocks(-c_im.astype(F32)))


def _s5_input_proj(u_bf16, br_ref, bi_ref, store_re, store_im):
    cw, sw = br_ref.shape[1], br_ref.shape[2]
    for k in range(SSM_BLOCKS):
        uk = u_bf16[:, k * cw:(k + 1) * cw]
        store_re(k * sw, sw, jnp.dot(uk, br_ref[k], preferred_element_type=F32))
        store_im(k * sw, sw, jnp.dot(uk, bi_ref[k], preferred_element_type=F32))


def _s5_output(hs_re, hs_im, u, cr_ref, ci_ref, d_ref, gw_ref, gb_ref):
    sw, cw = cr_ref.shape[1], cr_ref.shape[2]
    ys = []
    for k in range(SSM_BLOCKS):
        ys.append(jnp.dot(hs_re(k * sw, sw).astype(BF16), cr_ref[k], preferred_element_type=F32)
                  + jnp.dot(hs_im(k * sw, sw).astype(BF16), ci_ref[k], preferred_element_type=F32))
    y = jnp.concatenate(ys, axis=-1) + d_ref[...] * u
    zz = jax.nn.gelu(y, approximate=True)
    gate = jnp.dot(zz.astype(BF16), gw_ref[...], preferred_element_type=F32) + gb_ref[...]
    return zz * jax.nn.sigmoid(gate)


def _s5_kernel(u_ref, h0r_ref, h0i_ref, ar_ref, ai_ref, br_ref, bi_ref, cr_ref, ci_ref, d_ref, gw_ref, gb_ref,
               o_ref, hr_ref, hi_ref, xr_sc, xi_sc, io_sc, *, nb, lane_chunk):
    rows, n_state = xr_sc.shape
    steps = SUBLANES // nb
    t_chunk = rows // nb
    n_slab = io_sc.shape[0]

    @pl.when(pl.program_id(0) == 0)
    def _():
        hr_ref[...] = h0r_ref[...]
        hi_ref[...] = h0i_ref[...]

    for b in range(nb):
        for c in range(n_slab):
            io_sc[c, pl.ds(b, t_chunk, stride=nb), :] = u_ref[b, :, c * LANES:(c + 1) * LANES]
    u = jnp.concatenate([io_sc[c] for c in range(n_slab)], axis=-1)

    def store(sc):
        def f(c0, w, val):
            sc[:, c0:c0 + w] = val
        return f

    _s5_input_proj(u.astype(BF16), br_ref, bi_ref, store(xr_sc), store(xi_sc))

    row = lax.broadcasted_iota(jnp.int32, (SUBLANES, lane_chunk), 0)
    for c0 in range(0, n_state, lane_chunk):
        cols = slice(c0, c0 + lane_chunk)
        ar = jnp.broadcast_to(ar_ref[:, cols], (SUBLANES, lane_chunk))
        ai = jnp.broadcast_to(ai_ref[:, cols], (SUBLANES, lane_chunk))

        def tile(i, carry, cols=cols, ar=ar, ai=ai):
            hr, hi = carry
            r0 = pl.multiple_of(i * SUBLANES, SUBLANES)
            xr = xr_sc[pl.ds(r0, SUBLANES), cols]
            xi = xi_sc[pl.ds(r0, SUBLANES), cols]
            out_r, out_i = xr, xi
            for k in range(steps):
                pr = pltpu.roll(hr, nb, 0) if nb < SUBLANES else hr
                pi = pltpu.roll(hi, nb, 0) if nb < SUBLANES else hi
                hr = ar * pr - ai * pi + xr
                hi = ar * pi + ai * pr + xi
                sel = (row >= k * nb) & (row < (k + 1) * nb)
                out_r = jnp.where(sel, hr, out_r)
                out_i = jnp.where(sel, hi, out_i)
            xr_sc[pl.ds(r0, SUBLANES), cols] = out_r
            xi_sc[pl.ds(r0, SUBLANES), cols] = out_i
            return hr, hi

        hr, hi = lax.fori_loop(0, rows // SUBLANES, tile, (hr_ref[:, cols], hi_ref[:, cols]))
        hr_ref[:, cols] = hr
        hi_ref[:, cols] = hi

    out = _s5_output(lambda c0, w: xr_sc[:, c0:c0 + w], lambda c0, w: xi_sc[:, c0:c0 + w],
                     u, cr_ref, ci_ref, d_ref, gw_ref, gb_ref)
    for c in range(n_slab):
        io_sc[c] = out[:, c * LANES:(c + 1) * LANES]
    for b in range(nb):
        for c in range(n_slab):
            o_ref[b, :, c * LANES:(c + 1) * LANES] = io_sc[c, pl.ds(b, t_chunk, stride=nb), :].astype(o_ref.dtype)


def s5_mixer_prompt(z3, u_col0, h0_re, h0_im, prm, *, t_chunk):
    nb, seq, _ = z3.shape
    a_re, a_im, b_re, b_im, c_re, c_im, d, glu_w, glu_b = prm
    n_state = a_re.shape[1]
    cw = d.shape[1]
    rows = t_chunk * nb
    const2 = lambda i: (0, 0)
    const3 = lambda i: (0, 0, 0)
    return pl.pallas_call(
        functools.partial(_s5_kernel, nb=nb, lane_chunk=512),
        grid=(seq // t_chunk,),
        in_specs=[pl.BlockSpec((nb, t_chunk, cw), lambda i: (0, i, u_col0 // cw)),
                  pl.BlockSpec((SUBLANES, n_state), const2), pl.BlockSpec((SUBLANES, n_state), const2),
                  pl.BlockSpec((1, n_state), const2), pl.BlockSpec((1, n_state), const2),
                  pl.BlockSpec(b_re.shape, const3), pl.BlockSpec(b_im.shape, const3),
                  pl.BlockSpec(c_re.shape, const3), pl.BlockSpec(c_im.shape, const3),
                  pl.BlockSpec((1, cw), const2), pl.BlockSpec((cw, cw), const2), pl.BlockSpec((1, cw), const2)],
        out_specs=[pl.BlockSpec((nb, t_chunk, cw), lambda i: (0, i, 0)),
                   pl.BlockSpec((SUBLANES, n_state), const2), pl.BlockSpec((SUBLANES, n_state), const2)],
        out_shape=[jax.ShapeDtypeStruct((nb, seq, cw), BF16),
                   jax.ShapeDtypeStruct((SUBLANES, n_state), F32),
                   jax.ShapeDtypeStruct((SUBLANES, n_state), F32)],
        scratch_shapes=[pltpu.VMEM((rows, n_state), F32), pltpu.VMEM((rows, n_state), F32),
                        pltpu.VMEM((cw // LANES, rows, LANES), F32)],
        compiler_params=_params("arbitrary"),
        name="s5_mixer",
    )(z3, h0_re, h0_im, a_re, a_im, b_re, b_im, c_re, c_im, d, glu_w, glu_b)


def _decode_mix_kernel(p_ref, u_ref, buf_ref, pw_ref, ps_ref, h0r_ref, h0i_ref, ar_ref, ai_ref,
                       br_ref, bi_ref, cr_ref, ci_ref, d_ref, gw_ref, gb_ref,
                       op_ref, nbuf_ref, os_ref, hr_ref, hi_ref, *, pos0, gw):
    p = p_ref[...]
    feats = []
    for g, w in enumerate(POOL_WINDOWS):
        cols = slice(g * gw, (g + 1) * gw)
        win = p[:, cols]
        for j in range(1, w):
            win = win + buf_ref[POOL_BUF - j, :, cols]
        feats.append(win / float(min(pos0 + 1, w)) - p[:, cols])
    for g, f in enumerate(feats):
        cols = slice(g * gw, (g + 1) * gw)
        o = jnp.dot(f.astype(BF16), pw_ref[g], preferred_element_type=F32) * ps_ref[:, cols]
        op_ref[:, cols] = o
    for r in range(POOL_BUF - 1):
        nbuf_ref[r] = buf_ref[r + 1]
    nbuf_ref[POOL_BUF - 1] = p

    u = u_ref[...]

    def store(ref):
        def f(c0, w, val):
            ref[:, c0:c0 + w] = val
        return f

    _s5_input_proj(u.astype(BF16), br_ref, bi_ref, store(hr_ref), store(hi_ref))
    h0r, h0i = h0r_ref[...], h0i_ref[...]
    ar, ai = ar_ref[...], ai_ref[...]
    hr = ar * h0r - ai * h0i + hr_ref[...]
    hi = ar * h0i + ai * h0r + hi_ref[...]
    hr_ref[...] = hr
    hi_ref[...] = hi
    os_ref[...] = _s5_output(lambda c0, w: hr_ref[:, c0:c0 + w], lambda c0, w: hi_ref[:, c0:c0 + w],
                             u, cr_ref, ci_ref, d_ref, gw_ref, gb_ref)


def decode_mixers(z, u_col0, buf_t, pool_w, pool_scale, h0_re, h0_im, s5prm, pos0):
    r = z.shape[0]
    ng, gw, _ = pool_w.shape
    pw = ng * gw
    a_re, a_im, b_re, b_im, c_re, c_im, d, glu_w, glu_b = s5prm
    cw = d.shape[1]
    n_state = a_re.shape[1]
    full = lambda a: pl.BlockSpec(a.shape, lambda i: (0,) * a.ndim)
    args = (buf_t, pool_w, pool_scale.reshape(1, pw), h0_re, h0_im, a_re, a_im, b_re, b_im, c_re, c_im,
            d, glu_w, glu_b)
    return pl.pallas_call(
        functools.partial(_decode_mix_kernel, pos0=pos0, gw=gw),
        grid=(1,),
        in_specs=[pl.BlockSpec((r, pw), lambda i: (0, 0)),
                  pl.BlockSpec((r, cw), lambda i: (0, u_col0 // cw))] + [full(a) for a in args],
        out_specs=[pl.BlockSpec((r, pw), lambda i: (0, 0)),
                   pl.BlockSpec((POOL_BUF, r, pw), lambda i: (0, 0, 0)),
                   pl.BlockSpec((r, cw), lambda i: (0, 0)),
                   pl.BlockSpec((r, n_state), lambda i: (0, 0)),
                   pl.BlockSpec((r, n_state), lambda i: (0, 0))],
        out_shape=[jax.ShapeDtypeStruct((r, pw), F32),
                   jax.ShapeDtypeStruct((POOL_BUF, r, pw), F32),
                   jax.ShapeDtypeStruct((r, cw), F32),
                   jax.ShapeDtypeStruct((r, n_state), F32),
                   jax.ShapeDtypeStruct((r, n_state), F32)],
        compiler_params=_params("arbitrary"),
        name="decode_mixers",
    )(z, z, *args)


def _lambda(lq1_ref, lk1_ref, lq2_ref, lk2_ref, lam_init):
    return (jnp.exp(jnp.sum(lq1_ref[...] * lk1_ref[...], axis=-1, keepdims=True))
            - jnp.exp(jnp.sum(lq2_ref[...] * lk2_ref[...], axis=-1, keepdims=True)) + lam_init)


def _diff_finish(o1, o2, lam, sw, lam_init):
    o = o1 - lam * o2
    return _rms(o, sw, SUBLN_EPS) * (1.0 - lam_init)


def _attn_kernel(q_ref, k_ref, v_ref, lq1_ref, lk1_ref, lq2_ref, lk2_ref, sw_ref, o_ref,
                 m_sc, l_sc, acc_sc, *, t, hd, scale, lam_init):
    i = pl.program_id(2)
    q = q_ref[...]
    c_exp = scale * math.log2(math.e)
    m_sc[...] = jnp.full(m_sc.shape, -jnp.inf, F32)
    l_sc[...] = jnp.zeros(l_sc.shape, F32)
    acc_sc[...] = jnp.zeros(acc_sc.shape, F32)

    def kv_tile(j, diagonal):
        r0 = pl.multiple_of(j * t, t)
        kt = k_ref[pl.ds(r0, t), :]
        vt = v_ref[pl.ds(r0, t), :]
        for mp in range(2):
            s = lax.dot_general(kt[:, mp * hd:(mp + 1) * hd], q[:, mp * hd:(mp + 1) * hd],
                                (((1,), (1,)), ((), ())), preferred_element_type=F32)
            if diagonal:
                kk = lax.broadcasted_iota(jnp.int32, (t, t), 0)
                qq = lax.broadcasted_iota(jnp.int32, (t, t), 1)
                s = jnp.where(kk <= qq, s, -jnp.inf)
            m_old = m_sc[mp]
            m_new = jnp.maximum(m_old, jnp.max(s, axis=0, keepdims=True))
            alpha = jnp.exp2((m_old - m_new) * c_exp)
            p = jnp.exp2((s - m_new) * c_exp)
            l_sc[mp] = alpha * l_sc[mp] + jnp.sum(p, axis=0, keepdims=True)
            pv = lax.dot_general(vt, p.astype(BF16), (((0,), (0,)), ((), ())), preferred_element_type=F32)
            acc_sc[mp] = alpha * acc_sc[mp] + pv
            m_sc[mp] = m_new

    def body(j, c):
        kv_tile(j, False)
        return c

    lax.fori_loop(0, i, body, 0)
    kv_tile(i, True)
    lam = _lambda(lq1_ref, lk1_ref, lq2_ref, lk2_ref, lam_init)
    o = acc_sc[0] / l_sc[0] - lam * (acc_sc[1] / l_sc[1])
    o = o * lax.rsqrt(jnp.mean(o * o, axis=0, keepdims=True) + SUBLN_EPS) * sw_ref[...] * (1.0 - lam_init)
    o_ref[...] = o.T.astype(o_ref.dtype)


def diff_attention_prompt(q, k, v, lam_prm, subln_w, *, batch, lam_init, t):
    m, width = q.shape
    s = m // batch
    hd = lam_prm[0].shape[-1]
    vd = 2 * hd
    nh = width // vd
    t = _tile(s, t)
    nq = s // t
    vec = lambda n: pl.BlockSpec((1, n), lambda b, h, i: (0, 0))
    return pl.pallas_call(
        functools.partial(_attn_kernel, t=t, hd=hd, scale=hd ** -0.5, lam_init=lam_init),
        grid=(batch, nh, nq),
        in_specs=[pl.BlockSpec((t, vd), lambda b, h, i: (b * nq + i, h)),
                  pl.BlockSpec((s, vd), lambda b, h, i: (b, h)),
                  pl.BlockSpec((s, vd), lambda b, h, i: (b, h)),
                  vec(hd), vec(hd), vec(hd), vec(hd), pl.BlockSpec((vd, 1), lambda b, h, i: (0, 0))],
        out_specs=pl.BlockSpec((t, vd), lambda b, h, i: (b * nq + i, h)),
        out_shape=jax.ShapeDtypeStruct((m, width), BF16),
        scratch_shapes=[pltpu.VMEM((2, 1, t), F32), pltpu.VMEM((2, 1, t), F32), pltpu.VMEM((2, vd, t), F32)],
        compiler_params=_params("parallel", "parallel", "arbitrary"),
        name="diff_attention",
    )(q, k, v, *[p.reshape(1, hd) for p in lam_prm], subln_w.reshape(vd, 1))


def _dec_attn_kernel(pt_ref, q_ref, *refs, n_pp, c_exp, lam_init):
    kc_refs, vc_refs = refs[:n_pp], refs[n_pp:2 * n_pp]
    kn_ref, vn_ref, lq1_ref, lk1_ref, lq2_ref, lk2_ref, sw_ref, o_ref, m_sc, l_sc, acc_sc = refs[2 * n_pp:]
    pg = pl.program_id(1)
    nh, hd = m_sc.shape[1], m_sc.shape[2]
    ones = jnp.ones((hd, hd), BF16)
    twice = lambda x: jnp.concatenate([x, x], axis=-1)

    @pl.when(pg == 0)
    def _():
        m_sc[...] = jnp.full(m_sc.shape, -jnp.inf, F32)
        l_sc[...] = jnp.zeros(l_sc.shape, F32)
        acc_sc[...] = jnp.zeros(acc_sc.shape, F32)

    def update(keys, values):
        for mp in range(2):
            qm = q_ref[pl.ds(mp, nh, stride=2), :] * c_exp
            s = []
            for k in keys:
                prod = k(mp) * qm[None]
                rows = prod.shape[0]
                rep = jnp.dot(prod.reshape(rows * nh, hd).astype(BF16), ones, preferred_element_type=F32)
                s.append(rep.reshape(rows, nh, hd))
            m_old = m_sc[mp]
            m_new = functools.reduce(jnp.maximum, [jnp.max(x, axis=0) for x in s], m_old)
            alpha = jnp.exp2(m_old - m_new)
            p = [jnp.exp2(x - m_new) for x in s]
            l_sc[mp] = alpha * l_sc[mp] + sum(jnp.sum(x, axis=0) for x in p)
            acc_sc[mp] = twice(alpha) * acc_sc[mp] + sum(jnp.sum(twice(x) * v, axis=0) for x, v in zip(p, values))
            m_sc[mp] = m_new

    update([lambda mp, r=r: r[:, pl.ds(mp, nh, stride=2), :] for r in kc_refs], [r[...] for r in vc_refs])

    @pl.when(pg == pl.num_programs(1) - 1)
    def _():
        update([lambda mp: kn_ref[pl.ds(mp, nh, stride=2), :][None]], [vn_ref[...][None]])
        lam = _lambda(lq1_ref, lk1_ref, lq2_ref, lk2_ref, lam_init)
        o_ref[...] = _diff_finish(acc_sc[0] / twice(l_sc[0]), acc_sc[1] / twice(l_sc[1]), lam, sw_ref[...],
                                  lam_init)


def diff_attention_decode(q, k_new, v_new, cache_k, cache_v, layer, page_table, lam_prm, subln_w, *, lam_init, n_pp):
    r, nj, hd = q.shape
    nh, vd = v_new.shape[1], v_new.shape[2]
    n_pages = page_table.shape[1]
    page = cache_k.shape[2]
    assert n_pages % n_pp == 0
    vec = lambda n: pl.BlockSpec((1, n), lambda b, p, pt: (0, 0))
    row_map = lambda b, p, pt: (b, 0, 0)

    def page_spec(heads, dim, t):
        return pl.BlockSpec((None, None, page, heads, dim),
                            lambda b, p, pt: (layer, pt[b * n_pages + p * n_pp + t], 0, 0, 0))

    grid_spec = pltpu.PrefetchScalarGridSpec(
        num_scalar_prefetch=1,
        grid=(r, n_pages // n_pp),
        in_specs=[pl.BlockSpec((None, nj, hd), row_map)]
        + [page_spec(nj, hd, t) for t in range(n_pp)] + [page_spec(nh, vd, t) for t in range(n_pp)]
        + [pl.BlockSpec((None, nj, hd), row_map), pl.BlockSpec((None, nh, vd), row_map),
           vec(hd), vec(hd), vec(hd), vec(hd), vec(vd)],
        out_specs=pl.BlockSpec((None, nh, vd), row_map),
        scratch_shapes=[pltpu.VMEM((2, nh, hd), F32), pltpu.VMEM((2, nh, hd), F32), pltpu.VMEM((2, nh, vd), F32)],
    )
    return pl.pallas_call(
        functools.partial(_dec_attn_kernel, n_pp=n_pp, c_exp=hd ** -0.5 * math.log2(math.e), lam_init=lam_init),
        grid_spec=grid_spec,
        out_shape=jax.ShapeDtypeStruct((r, nh, vd), F32),
        compiler_params=_params("parallel", "arbitrary"),
        name="diff_attention_decode",
    )(page_table.reshape(-1), q, *([cache_k] * n_pp), *([cache_v] * n_pp), k_new, v_new,
      *[p.reshape(1, hd) for p in lam_prm], subln_w.reshape(1, vd))


def kernel(x_prompt, x_sample, cache_k, cache_v, state_pool, state_ssm_re, state_ssm_im, page_table, norm_mix_pre, norm_mix_post, norm_ffn_pre, norm_ffn_post, w_in, pool_w, pool_scale, lambda_q1, lambda_k1, lambda_q2, lambda_k2, subln_w, ssm_a_re, ssm_a_im, ssm_b_re, ssm_b_im, ssm_c_re, ssm_c_im, ssm_d, ssm_log_dt, glu_w, glu_b, w_lift, w_out, ffn_w_gate, ffn_w_up, ffn_w_down):
    bp, seq, d_model = x_prompt.shape
    db, dseq, _ = x_sample.shape
    assert dseq == 1
    depth = w_in.shape[0]
    page = cache_k.shape[2]
    n_past = page_table.shape[1] * page
    n_heads2, hd = cache_k.shape[3], cache_k.shape[4]
    n_heads, vd = cache_v.shape[3], cache_v.shape[4]
    attn_qw = n_heads2 * hd
    attn_vw = n_heads * vd
    pool_width = pool_w.shape[1] * pool_w.shape[2]
    ssm_w = ssm_d.shape[1]
    n_groups, n_st = ssm_a_re.shape[1], ssm_a_re.shape[2]
    n_state = n_groups * n_st
    d_ff = ffn_w_gate.shape[-1]
    in_width = w_in.shape[2]
    q_col0 = pool_width
    u_col0 = pool_width + 2 * attn_qw + attn_vw
    gate_col0 = u_col0 + ssm_w
    tn = 512
    assert SUBLANES % bp == 0 and attn_qw == attn_vw and u_col0 % ssm_w == 0 and ssm_w % LANES == 0
    assert gate_col0 % tn == 0 and in_width - gate_col0 == 3 * d_model

    mp = bp * seq
    rs = BF16_ROWS
    xp = x_prompt.reshape(mp, d_model)
    xs = jnp.pad(x_sample.reshape(db, d_model), ((0, rs - db), (0, 0)))
    pad_s = lambda a: jnp.pad(a, ((0, rs - db),) + ((0, 0),) * (a.ndim - 1))

    cos_p, sin_p = _rope_tables(jnp.arange(seq, dtype=jnp.int32), hd)
    cos_s, sin_s = _rope_tables(jnp.full((rs,), n_past, jnp.int32), hd)

    hp = rmsnorm_bf16(xp, norm_mix_pre[0])
    hs = rmsnorm_bf16(xs, norm_mix_pre[0])
    outs = {k: [] for k in ("pp", "rp", "ip", "ks", "vs", "ps", "rs", "is")}
    t_seq = _tile(seq, 1024)
    t_lift = _tile(seq, 512)
    ident = lambda j, i: (i, j)
    k_all = v_all = None

    for l in range(depth):
        lam_init = 0.8 - 0.6 * math.exp(-0.3 * l)
        a_re, a_im, b_re, b_im, c_re, c_im = _s5_discretize(
            ssm_a_re[l], ssm_a_im[l], ssm_log_dt[l], ssm_b_re[l], ssm_b_im[l], ssm_c_re[l], ssm_c_im[l])
        s5prm = (a_re, a_im, b_re, b_im, c_re, c_im, ssm_d[l].reshape(1, ssm_w), glu_w[l].astype(BF16),
                 glu_b[l].reshape(1, ssm_w))
        lam_prm = (lambda_q1[l], lambda_k1[l], lambda_q2[l], lambda_k2[l])
        w_next = norm_mix_pre[l + 1] if l + 1 < depth else None
        pool_w_l = pool_w[l].astype(BF16)

        zs, z = project(hs, hp, w_in, l, tm=t_seq, tn=tn, n_blocks=gate_col0 // tn, w_col=lambda j: j,
                        s_shape=(rs, gate_col0), s_map=lambda j: (0, j), p_shape=(mp, gate_col0), p_map=ident)
        n_gate = in_width - gate_col0
        gs, gp = project(hs, hp, w_in, l, tm=t_seq, tn=tn, n_blocks=n_gate // tn,
                         w_col=lambda j: gate_col0 // tn + j, s_shape=(rs, n_gate), s_map=lambda j: (0, j),
                         p_shape=(mp, n_gate), p_map=ident, out_dtype=BF16, gate=True)
        z3 = z.reshape(bp, seq, gate_col0)

        q, k, k_all, v, v_all = rope_split(z, cos_p, sin_p, q_col0=q_col0, width=attn_qw, head_dim=hd, tm=512,
                                           stack=(l, depth, vd, k_all, v_all))
        o_pool, pbuf = pool_mixer_prompt(z3, pool_w_l, pool_scale[l])
        o_attn = diff_attention_prompt(q, k, v, lam_prm, subln_w[l], batch=bp, lam_init=lam_init, t=ATTN_TILE)
        zero_h = jnp.zeros((SUBLANES, n_state), F32)
        o_ssm, hre, him = s5_mixer_prompt(z3, u_col0, zero_h, zero_h, s5prm, t_chunk=S5_TIME_CHUNK)

        qs, _, kfs, _, vfs = rope_split(zs, cos_s, sin_s, q_col0=q_col0, width=attn_qw, head_dim=hd, tm=rs)
        buf_t = pad_s(state_pool[l]).transpose(1, 0, 2)
        o_pool_s, nbuf_t, o_ssm_s, sre, sim = decode_mixers(
            zs, u_col0, buf_t, pool_w_l, pool_scale[l],
            pad_s(state_ssm_re[l].reshape(db, n_state)), pad_s(state_ssm_im[l].reshape(db, n_state)),
            s5prm, n_past)
        o_attn_s = diff_attention_decode(
            qs[:db].astype(F32).reshape(db, n_heads2, hd), kfs[:db].reshape(db, n_heads2, hd),
            vfs[:db].reshape(db, n_heads, vd), cache_k, cache_v, l, page_table, lam_prm, subln_w[l],
            lam_init=lam_init, n_pp=math.gcd(DECODE_PAGES_PER_STEP, page_table.shape[1]))

        branches_s = (o_pool_s.astype(BF16), pad_s(o_attn_s.reshape(db, attn_vw)).astype(BF16), o_ssm_s.astype(BF16))
        branches_p = (o_pool.reshape(mp, pool_width), o_attn, o_ssm.reshape(mp, ssm_w))
        merged_s, merged = lift_merge(branches_s, gs, branches_p, gp, w_lift, l, tm=t_lift, tn=tn)
        ys, y = project(merged_s, merged, w_out, l, tm=t_seq, tn=tn, n_blocks=d_model // tn, w_col=lambda j: j,
                        s_shape=(rs, d_model), s_map=lambda j: (0, j), p_shape=(mp, d_model), p_map=ident)
        xs, h2s = residual_norm(xs, ys, norm_mix_post[l], norm_ffn_pre[l])
        xp, h2 = residual_norm(xp, y, norm_mix_post[l], norm_ffn_pre[l])
        act_s, act = swiglu(h2s, h2, ffn_w_gate, ffn_w_up, l, tm=1024, tn=256)
        w_down = ffn_w_down[l].astype(BF16)
        fs = matmul_ksplit(act_s, w_down, tm=rs, tn=512, tk=d_ff // 2)
        f = matmul_ksplit(act, w_down, tm=1024, tn=512, tk=d_ff // 2)
        xs, hs = residual_norm(xs, fs, norm_ffn_post[l], w_next)
        xp, hp = residual_norm(xp, f, norm_ffn_post[l], w_next)

        outs["pp"].append(pbuf)
        outs["rp"].append(hre[SUBLANES - bp:].reshape(bp, n_groups, n_st))
        outs["ip"].append(him[SUBLANES - bp:].reshape(bp, n_groups, n_st))
        outs["ks"].append(kfs[:db].reshape(db, 1, n_heads2, hd))
        outs["vs"].append(vfs[:db].reshape(db, 1, n_heads, vd))
        outs["ps"].append(nbuf_t.transpose(1, 0, 2)[:db])
        outs["rs"].append(sre[:db].reshape(db, n_groups, n_st))
        outs["is"].append(sim[:db].reshape(db, n_groups, n_st))

    st = lambda key: jnp.stack(outs[key])
    return (xp.reshape(bp, seq, d_model), xs[:db].reshape(db, 1, d_model),
            k_all.reshape(depth, bp, seq, n_heads2, hd), v_all.reshape(depth, bp, seq, n_heads, vd),
            st("pp"), st("rp"), st("ip"),
            st("ks"), st("vs"), st("ps"), st("rs"), st("is"))
```

```python
import functools
import math

import jax
import jax.numpy as jnp
from jax import lax
from jax.experimental import pallas as pl
from jax.experimental.pallas import tpu as pltpu

F32 = jnp.float32
BF16 = jnp.bfloat16

V7X_VMEM_BYTES = 64 * 1024 * 1024
VMEM_LIMIT_BYTES = V7X_VMEM_BYTES - 8 * 1024 * 1024
SUBLANES = 8
LANES = 128
BF16_ROWS = 16

NORM_EPS = 1e-6
SUBLN_EPS = 1e-5
ROPE_THETA = 10000.0
POOL_WINDOWS = (2, 4, 8, 16)
POOL_BUF = max(POOL_WINDOWS) - 1
SSM_BLOCKS = 4
ATTN_TILE = 1024
S5_TIME_CHUNK = 64
DECODE_PAGES_PER_STEP = 8
CAST_CHUNK = 1024


def _params(*sem):
    return pltpu.CompilerParams(dimension_semantics=sem, vmem_limit_bytes=VMEM_LIMIT_BYTES)


def _tile(n, pref):
    t = min(n, pref)
    while n % t:
        t //= 2
    return t


def _rms(x, w, eps):
    return x * lax.rsqrt(jnp.mean(x * x, axis=-1, keepdims=True) + eps) * w


def _rmsnorm_kernel(x_ref, w_ref, o_ref):
    o_ref[...] = _rms(x_ref[...], w_ref[...], NORM_EPS).astype(o_ref.dtype)


def rmsnorm_bf16(x, w):
    m, d = x.shape
    tm = _tile(m, 256)
    return pl.pallas_call(
        _rmsnorm_kernel,
        grid=(m // tm,),
        in_specs=[pl.BlockSpec((tm, d), lambda i: (i, 0)), pl.BlockSpec((1, d), lambda i: (0, 0))],
        out_specs=pl.BlockSpec((tm, d), lambda i: (i, 0)),
        out_shape=jax.ShapeDtypeStruct((m, d), BF16),
        compiler_params=_params("parallel"),
        name="rmsnorm",
    )(x, w.reshape(1, d))


def _resnorm_kernel(x_ref, y_ref, wpost_ref, wnext_ref, xo_ref, ho_ref):
    x = x_ref[...] + _rms(y_ref[...], wpost_ref[...], NORM_EPS)
    xo_ref[...] = x
    ho_ref[...] = _rms(x, wnext_ref[...], NORM_EPS).astype(ho_ref.dtype)


def _resnorm_last_kernel(x_ref, y_ref, wpost_ref, xo_ref):
    xo_ref[...] = x_ref[...] + _rms(y_ref[...], wpost_ref[...], NORM_EPS)


def residual_norm(x, y, w_post, w_next):
    m, d = x.shape
    tm = _tile(m, 256)
    row = pl.BlockSpec((tm, d), lambda i: (i, 0))
    vec = pl.BlockSpec((1, d), lambda i: (0, 0))
    if w_next is None:
        return pl.pallas_call(
            _resnorm_last_kernel, grid=(m // tm,), in_specs=[row, row, vec], out_specs=row,
            out_shape=jax.ShapeDtypeStruct((m, d), F32), compiler_params=_params("parallel"),
            name="resnorm_last",
        )(x, y, w_post.reshape(1, d)), None
    return pl.pallas_call(
        _resnorm_kernel, grid=(m // tm,), in_specs=[row, row, vec, vec], out_specs=[row, row],
        out_shape=[jax.ShapeDtypeStruct((m, d), F32), jax.ShapeDtypeStruct((m, d), BF16)],
        compiler_params=_params("parallel"), name="resnorm",
    )(x, y, w_post.reshape(1, d), w_next.reshape(1, d))


def _first_row_step():
    return pl.program_id(1) == 0


def _cast_chunks(w_ref, wbf_ref):
    k = w_ref.shape[0]
    kc = _tile(k, CAST_CHUNK)
    for c0 in range(0, k, kc):
        wb = w_ref[c0:c0 + kc, :].astype(BF16)
        wbf_ref[c0:c0 + kc, :] = wb
        yield slice(c0, c0 + kc), wb


def _proj_kernel(xs_ref, xp_ref, w_ref, os_ref, op_ref, wbf_ref, *, gate):
    post = jax.nn.sigmoid if gate else (lambda x: x)

    @pl.when(_first_row_step())
    def _():
        acc_s = acc_p = 0.0
        for rows, wb in _cast_chunks(w_ref, wbf_ref):
            acc_p = acc_p + jnp.dot(xp_ref[:, rows], wb, preferred_element_type=F32)
            acc_s = acc_s + jnp.dot(xs_ref[:, rows], wb, preferred_element_type=F32)
        os_ref[...] = post(acc_s).astype(os_ref.dtype)
        op_ref[...] = post(acc_p).astype(op_ref.dtype)

    @pl.when(jnp.logical_not(_first_row_step()))
    def _():
        op_ref[...] = post(jnp.dot(xp_ref[...], wbf_ref[...], preferred_element_type=F32)).astype(op_ref.dtype)


def project(xs, xp, w, layer, *, tm, tn, n_blocks, w_col, s_shape, s_map, p_shape, p_map, out_dtype=F32,
            gate=False):
    rs, k = xs.shape
    tm = _tile(xp.shape[0], tm)
    return pl.pallas_call(
        functools.partial(_proj_kernel, gate=gate),
        grid=(n_blocks, xp.shape[0] // tm),
        in_specs=[pl.BlockSpec((rs, k), lambda j, i: (0, 0)),
                  pl.BlockSpec((tm, k), lambda j, i: (i, 0)),
                  pl.BlockSpec((None, k, tn), lambda j, i: (layer, 0, w_col(j)))],
        out_specs=[pl.BlockSpec((rs, tn), lambda j, i: s_map(j)),
                   pl.BlockSpec((tm, tn), p_map)],
        out_shape=[jax.ShapeDtypeStruct(s_shape, out_dtype), jax.ShapeDtypeStruct(p_shape, out_dtype)],
        scratch_shapes=[pltpu.VMEM((k, tn), BF16)],
        compiler_params=_params("parallel", "arbitrary"),
        name="project",
    )(xs, xp, w)


def _silu_mul(g, u):
    return g * jax.nn.sigmoid(g) * u


def _swiglu_kernel(xs_ref, xp_ref, wg_ref, wu_ref, os_ref, op_ref, wgbf_ref, wubf_ref):
    @pl.when(_first_row_step())
    def _():
        gs = gp = us = up = 0.0
        for (rows, wg), (_, wu) in zip(_cast_chunks(wg_ref, wgbf_ref), _cast_chunks(wu_ref, wubf_ref)):
            xp, xs = xp_ref[:, rows], xs_ref[:, rows]
            gp = gp + jnp.dot(xp, wg, preferred_element_type=F32)
            up = up + jnp.dot(xp, wu, preferred_element_type=F32)
            gs = gs + jnp.dot(xs, wg, preferred_element_type=F32)
            us = us + jnp.dot(xs, wu, preferred_element_type=F32)
        os_ref[...] = _silu_mul(gs, us).astype(os_ref.dtype)
        op_ref[...] = _silu_mul(gp, up).astype(op_ref.dtype)

    @pl.when(jnp.logical_not(_first_row_step()))
    def _():
        x = xp_ref[...]
        g = jnp.dot(x, wgbf_ref[...], preferred_element_type=F32)
        u = jnp.dot(x, wubf_ref[...], preferred_element_type=F32)
        op_ref[...] = _silu_mul(g, u).astype(op_ref.dtype)


def swiglu(xs, xp, wg, wu, layer, *, tm, tn):
    rs, k = xs.shape
    m = xp.shape[0]
    n = wg.shape[2]
    tm = _tile(m, tm)
    wspec = pl.BlockSpec((None, k, tn), lambda j, i: (layer, 0, j))
    return pl.pallas_call(
        _swiglu_kernel,
        grid=(n // tn, m // tm),
        in_specs=[pl.BlockSpec((rs, k), lambda j, i: (0, 0)),
                  pl.BlockSpec((tm, k), lambda j, i: (i, 0)), wspec, wspec],
        out_specs=[pl.BlockSpec((rs, tn), lambda j, i: (0, j)),
                   pl.BlockSpec((tm, tn), lambda j, i: (i, j))],
        out_shape=[jax.ShapeDtypeStruct((rs, n), BF16), jax.ShapeDtypeStruct((m, n), BF16)],
        scratch_shapes=[pltpu.VMEM((k, tn), BF16), pltpu.VMEM((k, tn), BF16)],
        compiler_params=_params("parallel", "arbitrary"),
        name="swiglu",
    )(xs, xp, wg, wu)


def _merge(o_pool_ref, o_attn_ref, o_ssm_ref, g0_ref, g1_ref, g2_ref, wbf_ref):
    pw, aw = o_pool_ref.shape[1], o_attn_ref.shape[1]
    up = jnp.dot(o_pool_ref[...], wbf_ref[0:pw, :], preferred_element_type=F32)
    ua = jnp.dot(o_attn_ref[...], wbf_ref[pw:pw + aw, :], preferred_element_type=F32)
    us = jnp.dot(o_ssm_ref[...], wbf_ref[pw + aw:, :], preferred_element_type=F32)
    return g0_ref[...] * up + g1_ref[...] * ua + g2_ref[...] * us


def _lift_kernel(*refs):
    s_in, p_in, (w_ref, os_ref, op_ref, wbf_ref) = refs[0:6], refs[6:12], refs[12:]

    @pl.when(_first_row_step())
    def _():
        wbf_ref[...] = w_ref[...].astype(BF16)
        os_ref[...] = _merge(*s_in, wbf_ref).astype(os_ref.dtype)

    op_ref[...] = _merge(*p_in, wbf_ref).astype(op_ref.dtype)


def lift_merge(branches_s, gs, branches_p, gp, w_lift, layer, *, tm, tn):
    rs = gs.shape[0]
    m = gp.shape[0]
    k, d = w_lift.shape[1], w_lift.shape[2]
    tm = _tile(m, tm)
    gb, nb = 0, d // tn
    widths = [o.shape[1] for o in branches_s]
    s_specs = [pl.BlockSpec((rs, wd), lambda j, i: (0, 0)) for wd in widths]
    s_gates = [pl.BlockSpec((rs, tn), lambda j, i, br=br: (0, gb + br * nb + j)) for br in range(3)]
    p_specs = [pl.BlockSpec((tm, widths[0]), lambda j, i: (i, 0)),
               pl.BlockSpec((tm, widths[1]), lambda j, i: (i, 0)),
               pl.BlockSpec((tm, widths[2]), lambda j, i: (i, 0))]
    p_gates = [pl.BlockSpec((tm, tn), lambda j, i, br=br: (i, gb + br * nb + j)) for br in range(3)]
    return pl.pallas_call(
        _lift_kernel,
        grid=(nb, m // tm),
        in_specs=s_specs + s_gates + p_specs + p_gates
        + [pl.BlockSpec((None, k, tn), lambda j, i: (layer, 0, j))],
        out_specs=[pl.BlockSpec((rs, tn), lambda j, i: (0, j)), pl.BlockSpec((tm, tn), lambda j, i: (i, j))],
        out_shape=[jax.ShapeDtypeStruct((rs, d), BF16), jax.ShapeDtypeStruct((m, d), BF16)],
        scratch_shapes=[pltpu.VMEM((k, tn), BF16)],
        compiler_params=_params("parallel", "arbitrary"),
        name="lift_merge",
    )(*branches_s, gs, gs, gs, *branches_p, gp, gp, gp, w_lift)


def _mm_acc_kernel(x_ref, w_ref, o_ref, acc_ref):
    kk = pl.program_id(2)

    @pl.when(kk == 0)
    def _():
        acc_ref[...] = jnp.zeros_like(acc_ref)

    acc_ref[...] += jnp.dot(x_ref[...], w_ref[...], preferred_element_type=F32)

    @pl.when(kk == pl.num_programs(2) - 1)
    def _():
        o_ref[...] = acc_ref[...]


def matmul_ksplit(x, w, *, tm, tn, tk):
    m, k = x.shape
    n = w.shape[1]
    tm = _tile(m, tm)
    return pl.pallas_call(
        _mm_acc_kernel,
        grid=(m // tm, n // tn, k // tk),
        in_specs=[pl.BlockSpec((tm, tk), lambda i, j, kk: (i, kk)),
                  pl.BlockSpec((tk, tn), lambda i, j, kk: (kk, j))],
        out_specs=pl.BlockSpec((tm, tn), lambda i, j, kk: (i, j)),
        out_shape=jax.ShapeDtypeStruct((m, n), F32),
        scratch_shapes=[pltpu.VMEM((tm, tn), F32)],
        compiler_params=_params("parallel", "parallel", "arbitrary"),
        name="matmul_ksplit",
    )(x, w)


def _rope_tables(pos, head_dim):
    half = head_dim // 2
    inv = ROPE_THETA ** (-jnp.arange(half, dtype=F32) / half)
    ang = pos.astype(F32)[:, None] * inv[None, :]
    cos, sin = jnp.cos(ang), jnp.sin(ang)
    return jnp.concatenate([cos, cos], axis=-1), jnp.concatenate([-sin, sin], axis=-1)


def _rope_kernel(q_ref, k_ref, v_ref, cos_ref, sin_ref, *rest, head_dim, stacked):
    qo_ref, ko_ref, kf_ref, vo_ref, vf_ref = rest[-5:]
    cos, sin = cos_ref[...], sin_ref[...]
    n_heads = q_ref.shape[1] // head_dim

    def rot(x):
        return x * cos + pltpu.roll(x, head_dim // 2, 1) * sin

    for h in range(n_heads):
        cols = slice(h * head_dim, (h + 1) * head_dim)
        qo_ref[:, cols] = rot(q_ref[:, cols]).astype(qo_ref.dtype)
        r = rot(k_ref[:, cols])
        ko_ref[:, cols] = r.astype(ko_ref.dtype)
        if stacked:
            kf_ref[:, h, :] = r
        else:
            kf_ref[:, cols] = r
    v = v_ref[...]
    vo_ref[...] = v.astype(vo_ref.dtype)
    if stacked:
        vd = vf_ref.shape[2]
        v_heads = v.shape[1] // vd
        h0 = pl.program_id(1) * v_heads
        for h in range(v_heads):
            vf_ref[:, pl.ds(h0 + h, 1), :] = v[:, None, h * vd:(h + 1) * vd]
    else:
        vf_ref[...] = v


def rope_split(z, cos, sin, *, q_col0, width, head_dim, tm, stack=None):
    m = z.shape[0]
    s_tab = cos.shape[0]
    tm = _tile(min(m, s_tab), tm)
    tc = 1024
    nc = width // tc
    qb = q_col0 // tc
    n_tab = s_tab // tm

    def zspec(off):
        return pl.BlockSpec((tm, tc), lambda i, j: (i, off + j))

    tab = pl.BlockSpec((tm, head_dim), lambda i, j: (i % n_tab, 0))
    out = pl.BlockSpec((tm, tc), lambda i, j: (i, j))
    in_specs = [zspec(qb), zspec(qb + nc), zspec(qb + 2 * nc), tab, tab]
    args = [z, z, z, cos, sin]
    aliases = {}
    if stack is None:
        f32_specs = [out, out]
        f32_shapes = [jax.ShapeDtypeStruct((m, width), F32)] * 2
    else:
        layer, depth, vd, k_all, v_all = stack
        kh, vh = tc // head_dim, width // vd
        f32_specs = [pl.BlockSpec((None, tm, kh, head_dim), lambda i, j: (layer, i, j, 0)),
                     pl.BlockSpec((None, tm, vh, vd), lambda i, j: (layer, i, 0, 0))]
        f32_shapes = [jax.ShapeDtypeStruct((depth, m, width // head_dim, head_dim), F32),
                      jax.ShapeDtypeStruct((depth, m, vh, vd), F32)]
        if k_all is not None:
            in_specs += [pl.BlockSpec(memory_space=pl.ANY)] * 2
            args += [k_all, v_all]
            aliases = {5: 2, 6: 4}
    return pl.pallas_call(
        functools.partial(_rope_kernel, head_dim=head_dim, stacked=stack is not None),
        grid=(m // tm, nc),
        in_specs=in_specs,
        out_specs=[out, out, f32_specs[0], out, f32_specs[1]],
        out_shape=[jax.ShapeDtypeStruct((m, width), BF16), jax.ShapeDtypeStruct((m, width), BF16), f32_shapes[0],
                   jax.ShapeDtypeStruct((m, width), BF16), f32_shapes[1]],
        input_output_aliases=aliases,
        compiler_params=_params("parallel", "arbitrary"),
        name="rope_split",
    )(*args)


def _pool_features(ext_load, p, pos0, gw):
    s = p.shape[0]
    pos = pos0 + lax.broadcasted_iota(jnp.int32, (s, 1), 0)
    feats = []
    for g, w in enumerate(POOL_WINDOWS):
        cols = slice(g * gw, (g + 1) * gw)
        win = p[:, cols]
        for j in range(1, w):
            win = win + ext_load(j, cols)
        cnt = jnp.minimum(pos + 1, w).astype(F32)
        feats.append(win / cnt - p[:, cols])
    return feats


def _pool_kernel(p_ref, w_ref, scale_ref, o_ref, buf_ref, ext_ref, *, s, gw):
    pad = BF16_ROWS
    ext_ref[0:pad, :] = jnp.zeros((pad, ext_ref.shape[1]), F32)
    ext_ref[pad:pad + s, :] = p_ref[...]
    p = p_ref[...]
    feats = _pool_features(lambda j, cols: ext_ref[pad - j:pad - j + s, cols], p, 0, gw)
    for g, f in enumerate(feats):
        cols = slice(g * gw, (g + 1) * gw)
        o = jnp.dot(f.astype(BF16), w_ref[g], preferred_element_type=F32) * scale_ref[:, cols]
        o_ref[:, cols] = o.astype(o_ref.dtype)
    buf_ref[...] = ext_ref[pad + s - POOL_BUF:pad + s, :]


def pool_mixer_prompt(z3, pool_w, pool_scale):
    b, s, _ = z3.shape
    ng, gw, _ = pool_w.shape
    width = ng * gw
    return pl.pallas_call(
        functools.partial(_pool_kernel, s=s, gw=gw),
        grid=(b,),
        in_specs=[pl.BlockSpec((None, s, width), lambda i: (i, 0, 0)),
                  pl.BlockSpec((ng, gw, gw), lambda i: (0, 0, 0)),
                  pl.BlockSpec((1, width), lambda i: (0, 0))],
        out_specs=[pl.BlockSpec((None, s, width), lambda i: (i, 0, 0)),
                   pl.BlockSpec((None, POOL_BUF, width), lambda i: (i, 0, 0))],
        out_shape=[jax.ShapeDtypeStruct((b, s, width), BF16),
                   jax.ShapeDtypeStruct((b, POOL_BUF, width), F32)],
        scratch_shapes=[pltpu.VMEM((BF16_ROWS + s, width), F32)],
        compiler_params=_params("parallel"),
        name="pool_mixer",
    )(z3, pool_w, pool_scale.reshape(1, width))


def _s5_discretize(a_re, a_im, log_dt, b_re, b_im, c_re, c_im):
    g, p = a_re.shape
    c = b_re.shape[-1]
    a = lax.complex(a_re.astype(F32), a_im.astype(F32))
    dt = jnp.exp(log_dt.astype(F32))[:, None]
    a_bar = jnp.exp(a * dt)
    b_bar = ((a_bar - 1.0) / a)[..., None] * lax.complex(b_re.astype(F32), b_im.astype(F32))
    gb = g // SSM_BLOCKS
    eye = jnp.eye(gb, dtype=F32)

    def b_blocks(x):
        x = x.reshape(SSM_BLOCKS, gb, p, c)
        return jnp.einsum('kgpc,gh->kgchp', x, eye).reshape(SSM_BLOCKS, gb * c, gb * p).astype(BF16)

    def c_blocks(x):
        x = x.reshape(SSM_BLOCKS, gb, c, p)
        return jnp.einsum('kgcp,gh->kgphc', x, eye).reshape(SSM_BLOCKS, gb * p, gb * c).astype(BF16)

    return (jnp.real(a_bar).reshape(1, g * p), jnp.imag(a_bar).reshape(1, g * p),
            b_blocks(jnp.real(b_bar)), b_blocks(jnp.imag(b_bar)),
            c_blocks(c_re.astype(F32)), c_blocks(-c_im.astype(F32)))


def _s5_input_proj(u_bf16, br_ref, bi_ref, store_re, store_im):
    cw, sw = br_ref.shape[1], br_ref.shape[2]
    for k in range(SSM_BLOCKS):
        uk = u_bf16[:, k * cw:(k + 1) * cw]
        store_re(k * sw, sw, jnp.dot(uk, br_ref[k], preferred_element_type=F32))
        store_im(k * sw, sw, jnp.dot(uk, bi_ref[k], preferred_element_type=F32))


def _s5_output(hs_re, hs_im, u, cr_ref, ci_ref, d_ref, gw_ref, gb_ref):
    sw, cw = cr_ref.shape[1], cr_ref.shape[2]
    ys = []
    for k in range(SSM_BLOCKS):
        ys.append(jnp.dot(hs_re(k * sw, sw).astype(BF16), cr_ref[k], preferred_element_type=F32)
                  + jnp.dot(hs_im(k * sw, sw).astype(BF16), ci_ref[k], preferred_element_type=F32))
    y = jnp.concatenate(ys, axis=-1) + d_ref[...] * u
    zz = jax.nn.gelu(y, approximate=True)
    gate = jnp.dot(zz.astype(BF16), gw_ref[...], preferred_element_type=F32) + gb_ref[...]
    return zz * jax.nn.sigmoid(gate)


def _s5_kernel(u_ref, h0r_ref, h0i_ref, ar_ref, ai_ref, br_ref, bi_ref, cr_ref, ci_ref, d_ref, gw_ref, gb_ref,
               o_ref, hr_ref, hi_ref, xr_sc, xi_sc, io_sc, *, nb, lane_chunk):
    rows, n_state = xr_sc.shape
    steps = SUBLANES // nb
    t_chunk = rows // nb
    n_slab = io_sc.shape[0]

    @pl.when(pl.program_id(0) == 0)
    def _():
        hr_ref[...] = h0r_ref[...]
        hi_ref[...] = h0i_ref[...]

    for b in range(nb):
        for c in range(n_slab):
            io_sc[c, pl.ds(b, t_chunk, stride=nb), :] = u_ref[b, :, c * LANES:(c + 1) * LANES]
    u = jnp.concatenate([io_sc[c] for c in range(n_slab)], axis=-1)

    def store(sc):
        def f(c0, w, val):
            sc[:, c0:c0 + w] = val
        return f

    _s5_input_proj(u.astype(BF16), br_ref, bi_ref, store(xr_sc), store(xi_sc))

    row = lax.broadcasted_iota(jnp.int32, (SUBLANES, lane_chunk), 0)
    for c0 in range(0, n_state, lane_chunk):
        cols = slice(c0, c0 + lane_chunk)
        ar = jnp.broadcast_to(ar_ref[:, cols], (SUBLANES, lane_chunk))
        ai = jnp.broadcast_to(ai_ref[:, cols], (SUBLANES, lane_chunk))

        def tile(i, carry, cols=cols, ar=ar, ai=ai):
            hr, hi = carry
            r0 = pl.multiple_of(i * SUBLANES, SUBLANES)
            xr = xr_sc[pl.ds(r0, SUBLANES), cols]
            xi = xi_sc[pl.ds(r0, SUBLANES), cols]
            out_r, out_i = xr, xi
            for k in range(steps):
                pr = pltpu.roll(hr, nb, 0) if nb < SUBLANES else hr
                pi = pltpu.roll(hi, nb, 0) if nb < SUBLANES else hi
                hr = ar * pr - ai * pi + xr
                hi = ar * pi + ai * pr + xi
                sel = (row >= k * nb) & (row < (k + 1) * nb)
                out_r = jnp.where(sel, hr, out_r)
                out_i = jnp.where(sel, hi, out_i)
            xr_sc[pl.ds(r0, SUBLANES), cols] = out_r
            xi_sc[pl.ds(r0, SUBLANES), cols] = out_i
            return hr, hi

        hr, hi = lax.fori_loop(0, rows // SUBLANES, tile, (hr_ref[:, cols], hi_ref[:, cols]))
        hr_ref[:, cols] = hr
        hi_ref[:, cols] = hi

    out = _s5_output(lambda c0, w: xr_sc[:, c0:c0 + w], lambda c0, w: xi_sc[:, c0:c0 + w],
                     u, cr_ref, ci_ref, d_ref, gw_ref, gb_ref)
    for c in range(n_slab):
        io_sc[c] = out[:, c * LANES:(c + 1) * LANES]
    for b in range(nb):
        for c in range(n_slab):
            o_ref[b, :, c * LANES:(c + 1) * LANES] = io_sc[c, pl.ds(b, t_chunk, stride=nb), :].astype(o_ref.dtype)


def s5_mixer_prompt(z3, u_col0, h0_re, h0_im, prm, *, t_chunk):
    nb, seq, _ = z3.shape
    a_re, a_im, b_re, b_im, c_re, c_im, d, glu_w, glu_b = prm
    n_state = a_re.shape[1]
    cw = d.shape[1]
    rows = t_chunk * nb
    const2 = lambda i: (0, 0)
    const3 = lambda i: (0, 0, 0)
    return pl.pallas_call(
        functools.partial(_s5_kernel, nb=nb, lane_chunk=512),
        grid=(seq // t_chunk,),
        in_specs=[pl.BlockSpec((nb, t_chunk, cw), lambda i: (0, i, u_col0 // cw)),
                  pl.BlockSpec((SUBLANES, n_state), const2), pl.BlockSpec((SUBLANES, n_state), const2),
                  pl.BlockSpec((1, n_state), const2), pl.BlockSpec((1, n_state), const2),
                  pl.BlockSpec(b_re.shape, const3), pl.BlockSpec(b_im.shape, const3),
                  pl.BlockSpec(c_re.shape, const3), pl.BlockSpec(c_im.shape, const3),
                  pl.BlockSpec((1, cw), const2), pl.BlockSpec((cw, cw), const2), pl.BlockSpec((1, cw), const2)],
        out_specs=[pl.BlockSpec((nb, t_chunk, cw), lambda i: (0, i, 0)),
                   pl.BlockSpec((SUBLANES, n_state), const2), pl.BlockSpec((SUBLANES, n_state), const2)],
        out_shape=[jax.ShapeDtypeStruct((nb, seq, cw), BF16),
                   jax.ShapeDtypeStruct((SUBLANES, n_state), F32),
                   jax.ShapeDtypeStruct((SUBLANES, n_state), F32)],
        scratch_shapes=[pltpu.VMEM((rows, n_state), F32), pltpu.VMEM((rows, n_state), F32),
                        pltpu.VMEM((cw // LANES, rows, LANES), F32)],
        compiler_params=_params("arbitrary"),
        name="s5_mixer",
    )(z3, h0_re, h0_im, a_re, a_im, b_re, b_im, c_re, c_im, d, glu_w, glu_b)


def _decode_mix_kernel(p_ref, u_ref, buf_ref, pw_ref, ps_ref, h0r_ref, h0i_ref, ar_ref, ai_ref,
                       br_ref, bi_ref, cr_ref, ci_ref, d_ref, gw_ref, gb_ref,
                       op_ref, nbuf_ref, os_ref, hr_ref, hi_ref, *, pos0, gw):
    p = p_ref[...]
    feats = []
    for g, w in enumerate(POOL_WINDOWS):
        cols = slice(g * gw, (g + 1) * gw)
        win = p[:, cols]
        for j in range(1, w):
            win = win + buf_ref[POOL_BUF - j, :, cols]
        feats.append(win / float(min(pos0 + 1, w)) - p[:, cols])
    for g, f in enumerate(feats):
        cols = slice(g * gw, (g + 1) * gw)
        o = jnp.dot(f.astype(BF16), pw_ref[g], preferred_element_type=F32) * ps_ref[:, cols]
        op_ref[:, cols] = o
    for r in range(POOL_BUF - 1):
        nbuf_ref[r] = buf_ref[r + 1]
    nbuf_ref[POOL_BUF - 1] = p

    u = u_ref[...]

    def store(ref):
        def f(c0, w, val):
            ref[:, c0:c0 + w] = val
        return f

    _s5_input_proj(u.astype(BF16), br_ref, bi_ref, store(hr_ref), store(hi_ref))
    h0r, h0i = h0r_ref[...], h0i_ref[...]
    ar, ai = ar_ref[...], ai_ref[...]
    hr = ar * h0r - ai * h0i + hr_ref[...]
    hi = ar * h0i + ai * h0r + hi_ref[...]
    hr_ref[...] = hr
    hi_ref[...] = hi
    os_ref[...] = _s5_output(lambda c0, w: hr_ref[:, c0:c0 + w], lambda c0, w: hi_ref[:, c0:c0 + w],
                             u, cr_ref, ci_ref, d_ref, gw_ref, gb_ref)


def decode_mixers(z, u_col0, buf_t, pool_w, pool_scale, h0_re, h0_im, s5prm, pos0):
    r = z.shape[0]
    ng, gw, _ = pool_w.shape
    pw = ng * gw
    a_re, a_im, b_re, b_im, c_re, c_im, d, glu_w, glu_b = s5prm
    cw = d.shape[1]
    n_state = a_re.shape[1]
    full = lambda a: pl.BlockSpec(a.shape, lambda i: (0,) * a.ndim)
    args = (buf_t, pool_w, pool_scale.reshape(1, pw), h0_re, h0_im, a_re, a_im, b_re, b_im, c_re, c_im,
            d, glu_w, glu_b)
    return pl.pallas_call(
        functools.partial(_decode_mix_kernel, pos0=pos0, gw=gw),
        grid=(1,),
        in_specs=[pl.BlockSpec((r, pw), lambda i: (0, 0)),
                  pl.BlockSpec((r, cw), lambda i: (0, u_col0 // cw))] + [full(a) for a in args],
        out_specs=[pl.BlockSpec((r, pw), lambda i: (0, 0)),
                   pl.BlockSpec((POOL_BUF, r, pw), lambda i: (0, 0, 0)),
                   pl.BlockSpec((r, cw), lambda i: (0, 0)),
                   pl.BlockSpec((r, n_state), lambda i: (0, 0)),
                   pl.BlockSpec((r, n_state), lambda i: (0, 0))],
        out_shape=[jax.ShapeDtypeStruct((r, pw), F32),
                   jax.ShapeDtypeStruct((POOL_BUF, r, pw), F32),
                   jax.ShapeDtypeStruct((r, cw), F32),
                   jax.ShapeDtypeStruct((r, n_state), F32),
                   jax.ShapeDtypeStruct((r, n_state), F32)],
        compiler_params=_params("arbitrary"),
        name="decode_mixers",
    )(z, z, *args)


def _lambda(lq1_ref, lk1_ref, lq2_ref, lk2_ref, lam_init):
    return (jnp.exp(jnp.sum(lq1_ref[...] * lk1_ref[...], axis=-1, keepdims=True))
            - jnp.exp(jnp.sum(lq2_ref[...] * lk2_ref[...], axis=-1, keepdims=True)) + lam_init)


def _diff_finish(o1, o2, lam, sw, lam_init):
    o = o1 - lam * o2
    return _rms(o, sw, SUBLN_EPS) * (1.0 - lam_init)


def _attn_kernel(q_ref, k_ref, v_ref, lq1_ref, lk1_ref, lq2_ref, lk2_ref, sw_ref, o_ref,
                 m_sc, l_sc, acc_sc, *, t, hd, scale, lam_init):
    i = pl.program_id(2)
    q = q_ref[...]
    c_exp = scale * math.log2(math.e)
    m_sc[...] = jnp.full(m_sc.shape, -jnp.inf, F32)
    l_sc[...] = jnp.zeros(l_sc.shape, F32)
    acc_sc[...] = jnp.zeros(acc_sc.shape, F32)

    def kv_tile(j, diagonal):
        r0 = pl.multiple_of(j * t, t)
        kt = k_ref[pl.ds(r0, t), :]
        vt = v_ref[pl.ds(r0, t), :]
        for mp in range(2):
            s = lax.dot_general(kt[:, mp * hd:(mp + 1) * hd], q[:, mp * hd:(mp + 1) * hd],
                                (((1,), (1,)), ((), ())), preferred_element_type=F32)
            if diagonal:
                kk = lax.broadcasted_iota(jnp.int32, (t, t), 0)
                qq = lax.broadcasted_iota(jnp.int32, (t, t), 1)
                s = jnp.where(kk <= qq, s, -jnp.inf)
            m_old = m_sc[mp]
            m_new = jnp.maximum(m_old, jnp.max(s, axis=0, keepdims=True))
            alpha = jnp.exp2((m_old - m_new) * c_exp)
            p = jnp.exp2((s - m_new) * c_exp)
            l_sc[mp] = alpha * l_sc[mp] + jnp.sum(p, axis=0, keepdims=True)
            pv = lax.dot_general(vt, p.astype(BF16), (((0,), (0,)), ((), ())), preferred_element_type=F32)
            acc_sc[mp] = alpha * acc_sc[mp] + pv
            m_sc[mp] = m_new

    def body(j, c):
        kv_tile(j, False)
        return c

    lax.fori_loop(0, i, body, 0)
    kv_tile(i, True)
    lam = _lambda(lq1_ref, lk1_ref, lq2_ref, lk2_ref, lam_init)
    o = acc_sc[0] / l_sc[0] - lam * (acc_sc[1] / l_sc[1])
    o = o * lax.rsqrt(jnp.mean(o * o, axis=0, keepdims=True) + SUBLN_EPS) * sw_ref[...] * (1.0 - lam_init)
    o_ref[...] = o.T.astype(o_ref.dtype)


def diff_attention_prompt(q, k, v, lam_prm, subln_w, *, batch, lam_init, t):
    m, width = q.shape
    s = m // batch
    hd = lam_prm[0].shape[-1]
    vd = 2 * hd
    nh = width // vd
    t = _tile(s, t)
    nq = s // t
    vec = lambda n: pl.BlockSpec((1, n), lambda b, h, i: (0, 0))
    return pl.pallas_call(
        functools.partial(_attn_kernel, t=t, hd=hd, scale=hd ** -0.5, lam_init=lam_init),
        grid=(batch, nh, nq),
        in_specs=[pl.BlockSpec((t, vd), lambda b, h, i: (b * nq + i, h)),
                  pl.BlockSpec((s, vd), lambda b, h, i: (b, h)),
                  pl.BlockSpec((s, vd), lambda b, h, i: (b, h)),
                  vec(hd), vec(hd), vec(hd), vec(hd), pl.BlockSpec((vd, 1), lambda b, h, i: (0, 0))],
        out_specs=pl.BlockSpec((t, vd), lambda b, h, i: (b * nq + i, h)),
        out_shape=jax.ShapeDtypeStruct((m, width), BF16),
        scratch_shapes=[pltpu.VMEM((2, 1, t), F32), pltpu.VMEM((2, 1, t), F32), pltpu.VMEM((2, vd, t), F32)],
        compiler_params=_params("parallel", "parallel", "arbitrary"),
        name="diff_attention",
    )(q, k, v, *[p.reshape(1, hd) for p in lam_prm], subln_w.reshape(vd, 1))


def _dec_attn_kernel(pt_ref, q_ref, *refs, n_pp, c_exp, lam_init):
    kc_refs, vc_refs = refs[:n_pp], refs[n_pp:2 * n_pp]
    kn_ref, vn_ref, lq1_ref, lk1_ref, lq2_ref, lk2_ref, sw_ref, o_ref, m_sc, l_sc, acc_sc = refs[2 * n_pp:]
    pg = pl.program_id(1)
    nh, hd = m_sc.shape[1], m_sc.shape[2]
    ones = jnp.ones((hd, hd), BF16)
    twice = lambda x: jnp.concatenate([x, x], axis=-1)

    @pl.when(pg == 0)
    def _():
        m_sc[...] = jnp.full(m_sc.shape, -jnp.inf, F32)
        l_sc[...] = jnp.zeros(l_sc.shape, F32)
        acc_sc[...] = jnp.zeros(acc_sc.shape, F32)

    def update(keys, values):
        for mp in range(2):
            qm = q_ref[pl.ds(mp, nh, stride=2), :] * c_exp
            s = []
            for k in keys:
                prod = k(mp) * qm[None]
                rows = prod.shape[0]
                rep = jnp.dot(prod.reshape(rows * nh, hd).astype(BF16), ones, preferred_element_type=F32)
                s.append(rep.reshape(rows, nh, hd))
            m_old = m_sc[mp]
            m_new = functools.reduce(jnp.maximum, [jnp.max(x, axis=0) for x in s], m_old)
            alpha = jnp.exp2(m_old - m_new)
            p = [jnp.exp2(x - m_new) for x in s]
            l_sc[mp] = alpha * l_sc[mp] + sum(jnp.sum(x, axis=0) for x in p)
            acc_sc[mp] = twice(alpha) * acc_sc[mp] + sum(jnp.sum(twice(x) * v, axis=0) for x, v in zip(p, values))
            m_sc[mp] = m_new

    update([lambda mp, r=r: r[:, pl.ds(mp, nh, stride=2), :] for r in kc_refs], [r[...] for r in vc_refs])

    @pl.when(pg == pl.num_programs(1) - 1)
    def _():
        update([lambda mp: kn_ref[pl.ds(mp, nh, stride=2), :][None]], [vn_ref[...][None]])
        lam = _lambda(lq1_ref, lk1_ref, lq2_ref, lk2_ref, lam_init)
        o_ref[...] = _diff_finish(acc_sc[0] / twice(l_sc[0]), acc_sc[1] / twice(l_sc[1]), lam, sw_ref[...],
                                  lam_init)


def diff_attention_decode(q, k_new, v_new, cache_k, cache_v, layer, page_table, lam_prm, subln_w, *, lam_init, n_pp):
    r, nj, hd = q.shape
    nh, vd = v_new.shape[1], v_new.shape[2]
    n_pages = page_table.shape[1]
    page = cache_k.shape[2]
    assert n_pages % n_pp == 0
    vec = lambda n: pl.BlockSpec((1, n), lambda b, p, pt: (0, 0))
    row_map = lambda b, p, pt: (b, 0, 0)

    def page_spec(heads, dim, t):
        return pl.BlockSpec((None, None, page, heads, dim),
                            lambda b, p, pt: (layer, pt[b * n_pages + p * n_pp + t], 0, 0, 0))

    grid_spec = pltpu.PrefetchScalarGridSpec(
        num_scalar_prefetch=1,
        grid=(r, n_pages // n_pp),
        in_specs=[pl.BlockSpec((None, nj, hd), row_map)]
        + [page_spec(nj, hd, t) for t in range(n_pp)] + [page_spec(nh, vd, t) for t in range(n_pp)]
        + [pl.BlockSpec((None, nj, hd), row_map), pl.BlockSpec((None, nh, vd), row_map),
           vec(hd), vec(hd), vec(hd), vec(hd), vec(vd)],
        out_specs=pl.BlockSpec((None, nh, vd), row_map),
        scratch_shapes=[pltpu.VMEM((2, nh, hd), F32), pltpu.VMEM((2, nh, hd), F32), pltpu.VMEM((2, nh, vd), F32)],
    )
    return pl.pallas_call(
        functools.partial(_dec_attn_kernel, n_pp=n_pp, c_exp=hd ** -0.5 * math.log2(math.e), lam_init=lam_init),
        grid_spec=grid_spec,
        out_shape=jax.ShapeDtypeStruct((r, nh, vd), F32),
        compiler_params=_params("parallel", "arbitrary"),
        name="diff_attention_decode",
    )(page_table.reshape(-1), q, *([cache_k] * n_pp), *([cache_v] * n_pp), k_new, v_new,
      *[p.reshape(1, hd) for p in lam_prm], subln_w.reshape(1, vd))


def kernel(x_prompt, x_sample, cache_k, cache_v, state_pool, state_ssm_re, state_ssm_im, page_table, norm_mix_pre, norm_mix_post, norm_ffn_pre, norm_ffn_post, w_in, pool_w, pool_scale, lambda_q1, lambda_k1, lambda_q2, lambda_k2, subln_w, ssm_a_re, ssm_a_im, ssm_b_re, ssm_b_im, ssm_c_re, ssm_c_im, ssm_d, ssm_log_dt, glu_w, glu_b, w_lift, w_out, ffn_w_gate, ffn_w_up, ffn_w_down):
    bp, seq, d_model = x_prompt.shape
    db, dseq, _ = x_sample.shape
    assert dseq == 1
    depth = w_in.shape[0]
    page = cache_k.shape[2]
    n_past = page_table.shape[1] * page
    n_heads2, hd = cache_k.shape[3], cache_k.shape[4]
    n_heads, vd = cache_v.shape[3], cache_v.shape[4]
    attn_qw = n_heads2 * hd
    attn_vw = n_heads * vd
    pool_width = pool_w.shape[1] * pool_w.shape[2]
    ssm_w = ssm_d.shape[1]
    n_groups, n_st = ssm_a_re.shape[1], ssm_a_re.shape[2]
    n_state = n_groups * n_st
    d_ff = ffn_w_gate.shape[-1]
    in_width = w_in.shape[2]
    q_col0 = pool_width
    u_col0 = pool_width + 2 * attn_qw + attn_vw
    gate_col0 = u_col0 + ssm_w
    tn = 512
    assert SUBLANES % bp == 0 and attn_qw == attn_vw and u_col0 % ssm_w == 0 and ssm_w % LANES == 0
    assert gate_col0 % tn == 0 and in_width - gate_col0 == 3 * d_model

    mp = bp * seq
    rs = BF16_ROWS
    xp = x_prompt.reshape(mp, d_model)
    xs = jnp.pad(x_sample.reshape(db, d_model), ((0, rs - db), (0, 0)))
    pad_s = lambda a: jnp.pad(a, ((0, rs - db),) + ((0, 0),) * (a.ndim - 1))

    cos_p, sin_p = _rope_tables(jnp.arange(seq, dtype=jnp.int32), hd)
    cos_s, sin_s = _rope_tables(jnp.full((rs,), n_past, jnp.int32), hd)

    hp = rmsnorm_bf16(xp, norm_mix_pre[0])
    hs = rmsnorm_bf16(xs, norm_mix_pre[0])
    outs = {k: [] for k in ("pp", "rp", "ip", "ks", "vs", "ps", "rs", "is")}
    t_seq = _tile(seq, 1024)
    t_lift = _tile(seq, 256)
    ident = lambda j, i: (i, j)
    k_all = v_all = None

    for l in range(depth):
        lam_init = 0.8 - 0.6 * math.exp(-0.3 * l)
        a_re, a_im, b_re, b_im, c_re, c_im = _s5_discretize(
            ssm_a_re[l], ssm_a_im[l], ssm_log_dt[l], ssm_b_re[l], ssm_b_im[l], ssm_c_re[l], ssm_c_im[l])
        s5prm = (a_re, a_im, b_re, b_im, c_re, c_im, ssm_d[l].reshape(1, ssm_w), glu_w[l].astype(BF16),
                 glu_b[l].reshape(1, ssm_w))
        lam_prm = (lambda_q1[l], lambda_k1[l], lambda_q2[l], lambda_k2[l])
        w_next = norm_mix_pre[l + 1] if l + 1 < depth else None
        pool_w_l = pool_w[l].astype(BF16)

        zs, z = project(hs, hp, w_in, l, tm=t_seq, tn=tn, n_blocks=gate_col0 // tn, w_col=lambda j: j,
                        s_shape=(rs, gate_col0), s_map=lambda j: (0, j), p_shape=(mp, gate_col0), p_map=ident)
        n_gate = in_width - gate_col0
        gs, gp = project(hs, hp, w_in, l, tm=t_seq // 2, tn=2 * tn, n_blocks=n_gate // (2 * tn),
                         w_col=lambda j: gate_col0 // (2 * tn) + j, s_shape=(rs, n_gate), s_map=lambda j: (0, j),
                         p_shape=(mp, n_gate), p_map=ident, out_dtype=BF16, gate=True)
        z3 = z.reshape(bp, seq, gate_col0)

        q, k, k_all, v, v_all = rope_split(z, cos_p, sin_p, q_col0=q_col0, width=attn_qw, head_dim=hd, tm=512,
                                           stack=(l, depth, vd, k_all, v_all))
        o_pool, pbuf = pool_mixer_prompt(z3, pool_w_l, pool_scale[l])
        o_attn = diff_attention_prompt(q, k, v, lam_prm, subln_w[l], batch=bp, lam_init=lam_init, t=ATTN_TILE)
        zero_h = jnp.zeros((SUBLANES, n_state), F32)
        o_ssm, hre, him = s5_mixer_prompt(z3, u_col0, zero_h, zero_h, s5prm, t_chunk=S5_TIME_CHUNK)

        qs, _, kfs, _, vfs = rope_split(zs, cos_s, sin_s, q_col0=q_col0, width=attn_qw, head_dim=hd, tm=rs)
        buf_t = pad_s(state_pool[l]).transpose(1, 0, 2)
        o_pool_s, nbuf_t, o_ssm_s, sre, sim = decode_mixers(
            zs, u_col0, buf_t, pool_w_l, pool_scale[l],
            pad_s(state_ssm_re[l].reshape(db, n_state)), pad_s(state_ssm_im[l].reshape(db, n_state)),
            s5prm, n_past)
        o_attn_s = diff_attention_decode(
            qs[:db].astype(F32).reshape(db, n_heads2, hd), kfs[:db].reshape(db, n_heads2, hd),
            vfs[:db].reshape(db, n_heads, vd), cache_k, cache_v, l, page_table, lam_prm, subln_w[l],
            lam_init=lam_init, n_pp=math.gcd(DECODE_PAGES_PER_STEP, page_table.shape[1]))

        branches_s = (o_pool_s.astype(BF16), pad_s(o_attn_s.reshape(db, attn_vw)).astype(BF16), o_ssm_s.astype(BF16))
        branches_p = (o_pool.reshape(mp, pool_width), o_attn, o_ssm.reshape(mp, ssm_w))
        merged_s, merged = lift_merge(branches_s, gs, branches_p, gp, w_lift, l, tm=t_lift, tn=2 * tn)
        ys, y = project(merged_s, merged, w_out, l, tm=t_seq, tn=tn, n_blocks=d_model // tn, w_col=lambda j: j,
                        s_shape=(rs, d_model), s_map=lambda j: (0, j), p_shape=(mp, d_model), p_map=ident)
        xs, h2s = residual_norm(xs, ys, norm_mix_post[l], norm_ffn_pre[l])
        xp, h2 = residual_norm(xp, y, norm_mix_post[l], norm_ffn_pre[l])
        act_s, act = swiglu(h2s, h2, ffn_w_gate, ffn_w_up, l, tm=1024, tn=256)
        w_down = ffn_w_down[l].astype(BF16)
        fs = matmul_ksplit(act_s, w_down, tm=rs, tn=512, tk=d_ff // 2)
        f = matmul_ksplit(act, w_down, tm=1024, tn=512, tk=d_ff // 2)
        xs, hs = residual_norm(xs, fs, norm_ffn_post[l], w_next)
        xp, hp = residual_norm(xp, f, norm_ffn_post[l], w_next)

        outs["pp"].append(pbuf)
        outs["rp"].append(hre[SUBLANES - bp:].reshape(bp, n_groups, n_st))
        outs["ip"].append(him[SUBLANES - bp:].reshape(bp, n_groups, n_st))
        outs["ks"].append(kfs[:db].reshape(db, 1, n_heads2, hd))
        outs["vs"].append(vfs[:db].reshape(db, 1, n_heads, vd))
        outs["ps"].append(nbuf_t.transpose(1, 0, 2)[:db])
        outs["rs"].append(sre[:db].reshape(db, n_groups, n_st))
        outs["is"].append(sim[:db].reshape(db, n_groups, n_st))

    st = lambda key: jnp.stack(outs[key])
    return (xp.reshape(bp, seq, d_model), xs[:db].reshape(db, 1, d_model),
            k_all.reshape(depth, bp, seq, n_heads2, hd), v_all.reshape(depth, bp, seq, n_heads, vd),
            st("pp"), st("rp"), st("ip"),
            st("ks"), st("vs"), st("ps"), st("rs"), st("is"))
```
